```python
import jax, jax.numpy as jnp
from jax import lax
import numpy as np

D_MODEL = 1024
BATCH = 32
SEQ = 2048
DEPTH = 1

RMS_EPS = 1e-5
HEAD_DIM = 64
ATT_HEADS = D_MODEL // HEAD_DIM
ATT_KV_HEADS = 4
WINDOW = 128
BLOCK = 128
ROT_DIM = HEAD_DIM // 4
ROPE_THETA = 500000.0
RWKV_HEAD_DIM = 64
RWKV_HEADS = D_MODEL // RWKV_HEAD_DIM
RWKV_WIDTH = RWKV_HEADS * RWKV_HEAD_DIM
DECAY_LORA = 64
AAA_LORA = 64
GATE_LORA = 128
RWKV_GN_EPS = 64e-5
MEM_LEN = 256
CROSS_HEADS = 4
CROSS_HEAD_DIM = 128
PEER_HEADS = 8
N_KEYS = 128
N_EXPERTS = N_KEYS * N_KEYS
PEER_QUERY_DIM = 256
PEER_TOPK = 16
PEER_CHUNK = 128
ATT_Q_WIDTH = ATT_HEADS * HEAD_DIM
ATT_KV_WIDTH = ATT_KV_HEADS * HEAD_DIM
RWKV_IN_WIDTH = 3 * RWKV_WIDTH + DECAY_LORA + AAA_LORA + GATE_LORA
GATE_WIDTH = 2 * D_MODEL
IN_WIDTH = ATT_Q_WIDTH + 2 * ATT_KV_WIDTH + RWKV_IN_WIDTH + GATE_WIDTH

kernel_name = 'hybrid_swa_rwkv7_peer_block'


def rms_norm(x, g):
    xf = x.astype(jnp.float32)
    y = xf * lax.rsqrt(jnp.mean(xf * xf, axis=-1, keepdims=True) + RMS_EPS)
    return (y * g.astype(jnp.float32)).astype(x.dtype)


def partial_rope(x, positions):
    half = ROT_DIM // 2
    inv_freq = 1.0 / (ROPE_THETA ** (jnp.arange(0, ROT_DIM, 2, dtype=jnp.float32) / ROT_DIM))
    ang = positions.astype(jnp.float32)[..., None] * inv_freq
    cos = jnp.cos(ang)[:, :, None, :]
    sin = jnp.sin(ang)[:, :, None, :]
    xr = x[..., :ROT_DIM].astype(jnp.float32)
    x1, x2 = xr[..., :half], xr[..., half:]
    rot = jnp.concatenate([x1 * cos - x2 * sin, x2 * cos + x1 * sin], axis=-1)
    return jnp.concatenate([rot.astype(x.dtype), x[..., ROT_DIM:]], axis=-1)


def sliding_window_attention(q, k, v, sinks):
    B, S = q.shape[0], q.shape[1]
    nb = S // BLOCK
    G = ATT_HEADS // ATT_KV_HEADS
    qb = q.reshape(B, nb, BLOCK, ATT_KV_HEADS, G, HEAD_DIM).transpose(1, 0, 2, 3, 4, 5)

    def band_keys(t):
        tb = t.reshape(B, nb, BLOCK, ATT_KV_HEADS, HEAD_DIM)
        prev = jnp.pad(tb, ((0, 0), (1, 0), (0, 0), (0, 0), (0, 0)))[:, :-1]
        return jnp.concatenate([prev, tb], axis=2).transpose(1, 0, 2, 3, 4)

    kw, vw = band_keys(k), band_keys(v)
    i = jnp.arange(BLOCK)[:, None]
    j = jnp.arange(2 * BLOCK)[None, :]
    band = (j > i) & (j <= i + WINDOW)
    sink = sinks.astype(jnp.float32).reshape(ATT_KV_HEADS, G)[None, :, :, None, None]
    scale = HEAD_DIM ** -0.5

    def one_block(args):
        qn, kn, vn, n = args
        s = jnp.einsum('bqkgd,bskd->bkgqs', qn.astype(jnp.float32), kn.astype(jnp.float32)) * scale
        valid = band & (n * BLOCK - BLOCK + j >= 0)
        s = jnp.where(valid, s, -1e30)
        m = jnp.maximum(jnp.max(s, axis=-1, keepdims=True), sink)
        p = jnp.exp(s - m)
        denom = jnp.sum(p, axis=-1, keepdims=True) + jnp.exp(sink - m)
        o = jnp.einsum('bkgqs,bskd->bqkgd', p / denom, vn.astype(jnp.float32))
        return o.astype(qn.dtype)

    out = lax.map(one_block, (qb, kw, vw, jnp.arange(nb)))
    return out.transpose(1, 0, 2, 3, 4, 5).reshape(B, S, ATT_HEADS * HEAD_DIM)


def rwkv7_time_mix(p, mu, w0, w2, a0, a2, g2, k_k, k_a, r_k, ln_w, ln_b):
    B, S = p.shape[0], p.shape[1]
    H, N, W = RWKV_HEADS, RWKV_HEAD_DIM, RWKV_WIDTH
    prev = jnp.pad(p, ((0, 0), (1, 0), (0, 0)))[:, :-1]
    p = p + (prev - p) * mu
    r = p[..., :W]
    k = p[..., W:2 * W]
    v = p[..., 2 * W:3 * W]
    dw = p[..., 3 * W:3 * W + DECAY_LORA]
    da = p[..., 3 * W + DECAY_LORA:3 * W + DECAY_LORA + AAA_LORA]
    dg = p[..., 3 * W + DECAY_LORA + AAA_LORA:]
    w_log = -jax.nn.softplus(-(w0 + jnp.tanh(dw) @ w2).astype(jnp.float32)) - 0.5
    decay = jnp.exp(-jnp.exp(w_log))
    a = jax.nn.sigmoid((a0 + da @ a2).astype(jnp.float32))
    g = jax.nn.sigmoid(dg) @ g2
    hs = lambda t: t.astype(jnp.float32).reshape(B, S, H, N)
    kk = hs(k * k_k)
    kk = kk / jnp.maximum(jnp.sqrt(jnp.sum(kk * kk, axis=-1, keepdims=True)), 1e-12)
    k = k.astype(jnp.float32) * (1.0 + (a - 1.0) * k_a.astype(jnp.float32))
    rh, kh, vh, wh, ah = hs(r), hs(k), hs(v), hs(decay), hs(a)

    def step(state, inp):
        r_t, w_t, k_t, v_t, kk_t, a_t = inp
        sa = jnp.einsum('bhij,bhj->bhi', state, -kk_t)
        state = (state * w_t[:, :, None, :]
                 + sa[..., :, None] * (kk_t * a_t)[:, :, None, :]
                 + v_t[..., :, None] * k_t[:, :, None, :])
        return state, jnp.einsum('bhij,bhj->bhi', state, r_t)

    tmaj = lambda t: t.transpose(1, 0, 2, 3)
    state0 = jnp.zeros((B, H, N, N), jnp.float32)
    _, o = lax.scan(step, state0, (tmaj(rh), tmaj(wh), tmaj(kh), tmaj(vh), tmaj(kk), tmaj(ah)))
    o = o.transpose(1, 0, 2, 3)
    mean = jnp.mean(o, axis=-1, keepdims=True)
    var = jnp.mean(jnp.square(o - mean), axis=-1, keepdims=True)
    on = ((o - mean) * lax.rsqrt(var + RWKV_GN_EPS)).reshape(B, S, W)
    on = on * ln_w.astype(jnp.float32) + ln_b.astype(jnp.float32)
    bonus = (jnp.sum(rh * kh * r_k.astype(jnp.float32), axis=-1, keepdims=True) * vh).reshape(B, S, W)
    return ((on + bonus) * g.astype(jnp.float32)).astype(p.dtype)


def memory_cross_attention(xn, mem_n, w_q, w_kv, w_o):
    B, S = xn.shape[0], xn.shape[1]
    M = mem_n.shape[1]
    q = (xn @ w_q).reshape(B, S, CROSS_HEADS, CROSS_HEAD_DIM)
    kv = (mem_n @ w_kv).reshape(B, M, 2, CROSS_HEADS, CROSS_HEAD_DIM)
    k, v = kv[:, :, 0], kv[:, :, 1]
    s = jnp.einsum('bshd,bmhd->bhsm', q.astype(jnp.float32), k.astype(jnp.float32)) * (CROSS_HEAD_DIM ** -0.5)
    pr = jax.nn.softmax(s, axis=-1)
    o = jnp.einsum('bhsm,bmhd->bshd', pr, v.astype(jnp.float32)).reshape(B, S, CROSS_HEADS * CROSS_HEAD_DIM)
    return o.astype(xn.dtype) @ w_o


def peer_ffn(xn, w_q, sub_keys_1, sub_keys_2, u, v):
    B, S, D = xn.shape
    half = PEER_QUERY_DIM // 2
    xc = xn.reshape((B * S) // PEER_CHUNK, PEER_CHUNK, D)
    k1 = sub_keys_1.astype(jnp.float32)
    k2 = sub_keys_2.astype(jnp.float32)

    def one_chunk(xt):
        q = (xt @ w_q).reshape(PEER_CHUNK, PEER_HEADS, PEER_QUERY_DIM).astype(jnp.float32)
        s1 = jnp.einsum('chd,nd->chn', q[..., :half], k1)
        s2 = jnp.einsum('chd,nd->chn', q[..., half:], k2)
        v1, i1 = lax.top_k(s1, PEER_TOPK)
        v2, i2 = lax.top_k(s2, PEER_TOPK)
        cand = (v1[..., :, None] + v2[..., None, :]).reshape(PEER_CHUNK, PEER_HEADS, PEER_TOPK * PEER_TOPK)
        sc, ci = lax.top_k(cand, PEER_TOPK)
        e = (jnp.take_along_axis(i1, ci // PEER_TOPK, axis=-1) * N_KEYS
             + jnp.take_along_axis(i2, ci % PEER_TOPK, axis=-1))
        gate = jax.nn.softmax(sc, axis=-1)
        act = jax.nn.gelu(jnp.einsum('cd,chkd->chk', xt, u[e]).astype(jnp.float32), approximate=False)
        return jnp.einsum('chk,chkd->cd', (gate * act).astype(xt.dtype), v[e])

    return lax.map(one_chunk, xc).reshape(B, S, D)


def hybrid_layer(h, mem, positions, g_mix, w_in, attn_sinks, rwkv_mu, rwkv_w0, rwkv_w2, rwkv_a0,
                 rwkv_a2, rwkv_g2, rwkv_k_k, rwkv_k_a, rwkv_r_k, rwkv_ln_w, rwkv_ln_b, w_out,
                 g_cross, g_mem, w_q_cross, w_kv_cross, w_o_cross, g_ffn, peer_w_q,
                 peer_sub_keys_1, peer_sub_keys_2, peer_u, peer_v):
    B, S = h.shape[0], h.shape[1]
    xn = rms_norm(h, g_mix)
    proj = xn @ w_in
    o1 = ATT_Q_WIDTH
    o2 = o1 + ATT_KV_WIDTH
    o3 = o2 + ATT_KV_WIDTH
    o4 = o3 + RWKV_IN_WIDTH
    q = partial_rope(proj[..., :o1].reshape(B, S, ATT_HEADS, HEAD_DIM), positions)
    k = partial_rope(proj[..., o1:o2].reshape(B, S, ATT_KV_HEADS, HEAD_DIM), positions)
    v = proj[..., o2:o3].reshape(B, S, ATT_KV_HEADS, HEAD_DIM)
    attn = sliding_window_attention(q, k, v, attn_sinks)
    rwkv = rwkv7_time_mix(proj[..., o3:o4], rwkv_mu, rwkv_w0, rwkv_w2, rwkv_a0, rwkv_a2, rwkv_g2,
                          rwkv_k_k, rwkv_k_a, rwkv_r_k, rwkv_ln_w, rwkv_ln_b)
    gates = jax.nn.sigmoid(proj[..., o4:])
    mixed = gates[..., :D_MODEL] * attn + gates[..., D_MODEL:] * rwkv
    h = h + mixed @ w_out
    h = h + memory_cross_attention(rms_norm(h, g_cross), rms_norm(mem, g_mem), w_q_cross, w_kv_cross, w_o_cross)
    h = h + peer_ffn(rms_norm(h, g_ffn), peer_w_q, peer_sub_keys_1, peer_sub_keys_2, peer_u, peer_v)
    return h


def setup_inputs(seed: int = 0) -> dict:
    key = jax.random.key(seed)
    ks = jax.random.split(key, 32)
    f32 = jnp.float32
    nrm = lambda i, shape, s: jax.random.normal(ks[i], shape, f32) * s
    L, D = DEPTH, D_MODEL
    half = PEER_QUERY_DIM // 2
    offsets = jax.random.randint(ks[2], (BATCH, 1), 0, 4096, dtype=jnp.int32)
    return {
        'x': nrm(0, (BATCH, SEQ, D), 1.0),
        'mem': nrm(1, (BATCH, MEM_LEN, D), 1.0),
        'positions': offsets + jnp.arange(SEQ, dtype=jnp.int32)[None, :],
        'g_mix': 1.0 + nrm(3, (L, D), 0.02),
        'w_in': nrm(4, (L, D, IN_WIDTH), D ** -0.5),
        'attn_sinks': nrm(5, (L, ATT_HEADS), 0.5),
        'rwkv_mu': jax.random.uniform(ks[6], (L, RWKV_IN_WIDTH), f32),
        'rwkv_w0': -0.5 + nrm(7, (L, RWKV_WIDTH), 0.5),
        'rwkv_w2': nrm(8, (L, DECAY_LORA, RWKV_WIDTH), 0.1),
        'rwkv_a0': nrm(9, (L, RWKV_WIDTH), 0.1),
        'rwkv_a2': nrm(10, (L, AAA_LORA, RWKV_WIDTH), 0.1),
        'rwkv_g2': nrm(11, (L, GATE_LORA, RWKV_WIDTH), GATE_LORA ** -0.5),
        'rwkv_k_k': 0.85 + nrm(12, (L, RWKV_WIDTH), 0.02),
        'rwkv_k_a': 1.0 + nrm(13, (L, RWKV_WIDTH), 0.02),
        'rwkv_r_k': nrm(14, (L, RWKV_HEADS, RWKV_HEAD_DIM), 0.1),
        'rwkv_ln_w': 1.0 + nrm(15, (L, RWKV_WIDTH), 0.02),
        'rwkv_ln_b': nrm(16, (L, RWKV_WIDTH), 0.02),
        'w_out': nrm(17, (L, D, D), D ** -0.5),
        'g_cross': 1.0 + nrm(18, (L, D), 0.02),
        'g_mem': 1.0 + nrm(19, (L, D), 0.02),
        'w_q_cross': nrm(20, (L, D, CROSS_HEADS * CROSS_HEAD_DIM), D ** -0.5),
        'w_kv_cross': nrm(21, (L, D, 2 * CROSS_HEADS * CROSS_HEAD_DIM), D ** -0.5),
        'w_o_cross': nrm(22, (L, CROSS_HEADS * CROSS_HEAD_DIM, D), (CROSS_HEADS * CROSS_HEAD_DIM) ** -0.5),
        'g_ffn': 1.0 + nrm(23, (L, D), 0.02),
        'peer_w_q': nrm(24, (L, D, PEER_HEADS * PEER_QUERY_DIM), D ** -0.5),
        'peer_sub_keys_1': nrm(25, (L, N_KEYS, half), half ** -0.5),
        'peer_sub_keys_2': nrm(26, (L, N_KEYS, half), half ** -0.5),
        'peer_u': nrm(27, (L, N_EXPERTS, D), D ** -0.5),
        'peer_v': nrm(28, (L, N_EXPERTS, D), PEER_HEADS ** -0.5),
        'g_final': 1.0 + nrm(29, (D,), 0.02),
    }


def reference(x, mem, positions, g_mix, w_in, attn_sinks, rwkv_mu, rwkv_w0, rwkv_w2, rwkv_a0,
              rwkv_a2, rwkv_g2, rwkv_k_k, rwkv_k_a, rwkv_r_k, rwkv_ln_w, rwkv_ln_b, w_out,
              g_cross, g_mem, w_q_cross, w_kv_cross, w_o_cross, g_ffn, peer_w_q,
              peer_sub_keys_1, peer_sub_keys_2, peer_u, peer_v, g_final):
    h = x
    for layer in range(DEPTH):
        h = hybrid_layer(h, mem, positions, g_mix[layer], w_in[layer], attn_sinks[layer],
                         rwkv_mu[layer], rwkv_w0[layer], rwkv_w2[layer], rwkv_a0[layer],
                         rwkv_a2[layer], rwkv_g2[layer], rwkv_k_k[layer], rwkv_k_a[layer],
                         rwkv_r_k[layer], rwkv_ln_w[layer], rwkv_ln_b[layer], w_out[layer],
                         g_cross[layer], g_mem[layer], w_q_cross[layer], w_kv_cross[layer],
                         w_o_cross[layer], g_ffn[layer], peer_w_q[layer],
                         peer_sub_keys_1[layer], peer_sub_keys_2[layer], peer_u[layer], peer_v[layer])
    return rms_norm(h, g_final)
```

```python
import functools
import math

import jax
import jax.numpy as jnp
from jax import lax
from jax.experimental import pallas as pl
from jax.experimental.pallas import tpu as pltpu

F32 = jnp.float32
BF16 = jnp.bfloat16

RMS_EPS = 1e-5
HEAD_DIM = 64
ATT_KV_HEADS = 4
WINDOW = 128
ROT_DIM = HEAD_DIM // 4
ROPE_THETA = 500000.0
RWKV_HEAD_DIM = 64
DECAY_LORA = 64
AAA_LORA = 64
GATE_LORA = 128
RWKV_GN_EPS = 64e-5
CROSS_HEADS = 4
PEER_HEADS = 8
PEER_TOPK = 16

LANES = 128
VMEM_LIMIT = 56 * 1024 * 1024


def _cparams(*sem):
    return pltpu.CompilerParams(dimension_semantics=sem, vmem_limit_bytes=VMEM_LIMIT)


def _const_spec(shape):
    nd = len(shape)
    return pl.BlockSpec(shape, lambda *_: (0,) * nd, pipeline_mode=pl.Buffered(1))


def _rms(x, g):
    return x * lax.rsqrt(jnp.mean(x * x, axis=-1, keepdims=True) + RMS_EPS) * g


def _sigmoid(x):
    return 1.0 / (1.0 + jnp.exp(-x))


def _rope(t, c, s):
    w = t.shape[-1]
    rep = w // LANES
    cc = jnp.tile(c, (1, rep))
    ss = jnp.tile(s, (1, rep))
    lane = lax.broadcasted_iota(jnp.int32, t.shape, 1)
    first = (lane % HEAD_DIM) < (ROT_DIM // 2)
    partner = jnp.where(first, pltpu.roll(t, w - ROT_DIM // 2, axis=1), pltpu.roll(t, ROT_DIM // 2, axis=1))
    return t * cc + partner * ss


def _inproj_kernel(x_ref, g_ref, w_ref, c_ref, s_ref, q_ref, k_ref, v_ref, p_ref, gate_ref, *, widths):
    wq, wk, wv, wp, wg = widths
    xb = _rms(x_ref[...], g_ref[...]).astype(BF16)
    c = c_ref[...]
    s = s_ref[...]
    o = 0
    q = jnp.dot(xb, w_ref[:, o:o + wq], preferred_element_type=F32)
    q_ref[...] = _rope(q, c, s).astype(q_ref.dtype)
    o += wq
    k = jnp.dot(xb, w_ref[:, o:o + wk], preferred_element_type=F32)
    k_ref[...] = _rope(k, c, s).astype(k_ref.dtype)
    o += wk
    v_ref[...] = jnp.dot(xb, w_ref[:, o:o + wv], preferred_element_type=F32).astype(v_ref.dtype)
    o += wv
    p_ref[...] = jnp.dot(xb, w_ref[:, o:o + wp], preferred_element_type=F32)
    o += wp
    gate_ref[...] = _sigmoid(jnp.dot(xb, w_ref[:, o:o + wg], preferred_element_type=F32)).astype(gate_ref.dtype)


def _in_proj(x2, g_mix, w_in_b, rope_c, rope_s, widths, tm):
    n, d = x2.shape
    wq, wk, wv, wp, wg = widths
    row = lambda w: pl.BlockSpec((tm, w), lambda i: (i, 0))
    return pl.pallas_call(
        functools.partial(_inproj_kernel, widths=widths),
        grid=(n // tm,),
        in_specs=[row(d), _const_spec((1, d)), _const_spec(w_in_b.shape), row(LANES), row(LANES)],
        out_specs=[row(wq), row(wk), row(wv), row(wp), row(wg)],
        out_shape=[jax.ShapeDtypeStruct((n, wq), BF16), jax.ShapeDtypeStruct((n, wk), BF16),
                   jax.ShapeDtypeStruct((n, wv), BF16), jax.ShapeDtypeStruct((n, wp), F32),
                   jax.ShapeDtypeStruct((n, wg), BF16)],
        compiler_params=_cparams("parallel"),
        name="in_proj",
    )(x2, g_mix.reshape(1, d), w_in_b, rope_c, rope_s)


def _rope_tables(positions):
    half = ROT_DIM // 2
    inv_freq = 1.0 / (ROPE_THETA ** (jnp.arange(0, ROT_DIM, 2, dtype=F32) / ROT_DIM))
    ang = positions.reshape(-1).astype(F32)[:, None] * inv_freq
    cos, sin = jnp.cos(ang), jnp.sin(ang)
    n = ang.shape[0]
    pad1 = jnp.ones((n, HEAD_DIM - ROT_DIM), F32)
    pad0 = jnp.zeros((n, HEAD_DIM - ROT_DIM), F32)
    c = jnp.concatenate([cos, cos, pad1], axis=1)
    s = jnp.concatenate([-sin, sin, pad0], axis=1)
    return jnp.tile(c, (1, LANES // HEAD_DIM)), jnp.tile(s, (1, LANES // HEAD_DIM))


def _swa_kernel(sink_ref, q_ref, kp_ref, kc_ref, vp_ref, vc_ref, o_ref, *, group):
    n = pl.program_id(1)
    blk = q_ref.shape[0]
    scale = HEAD_DIM ** -0.5
    rows = group * blk
    qi = lax.broadcasted_iota(jnp.int32, (rows, 2 * blk), 0) % blk
    kj = lax.broadcasted_iota(jnp.int32, (rows, 2 * blk), 1)
    valid = (kj > qi) & (kj <= qi + WINDOW) & ((kj >= blk) | (n > 0))
    rid = lax.broadcasted_iota(jnp.int32, (rows, 1), 0) // blk
    for kv in range(ATT_KV_HEADS):
        sl = slice(kv * HEAD_DIM, (kv + 1) * HEAD_DIM)
        kk = jnp.concatenate([kp_ref[:, sl], kc_ref[:, sl]], axis=0)
        vv = jnp.concatenate([vp_ref[:, sl], vc_ref[:, sl]], axis=0)
        heads = [kv * group + g for g in range(group)]
        qq = jnp.concatenate([q_ref[:, h * HEAD_DIM:(h + 1) * HEAD_DIM] for h in heads], axis=0)
        s = lax.dot_general(qq, kk, (((1,), (1,)), ((), ())), preferred_element_type=F32) * scale
        s = jnp.where(valid, s, -1e30)
        sink = jnp.zeros((rows, 1), F32)
        for g, h in enumerate(heads):
            sink = jnp.where(rid == g, sink_ref[h], sink)
        m = jnp.maximum(jnp.max(s, axis=-1, keepdims=True), sink)
        p = jnp.exp(s - m)
        denom = jnp.sum(p, axis=-1, keepdims=True) + jnp.exp(sink - m)
        o = jnp.dot((p / denom).astype(vv.dtype), vv, preferred_element_type=F32)
        for g, h in enumerate(heads):
            o_ref[:, h * HEAD_DIM:(h + 1) * HEAD_DIM] = o[g * blk:(g + 1) * blk].astype(o_ref.dtype)


def _swa(q, k, v, sinks, batch, seq):
    n, wq = q.shape
    wk = k.shape[1]
    blk = WINDOW
    nb = seq // blk
    group = (wq // HEAD_DIM) // ATT_KV_HEADS
    cur = lambda w: pl.BlockSpec((blk, w), lambda b, i: (b * nb + i, 0))
    prev = lambda w: pl.BlockSpec((blk, w), lambda b, i: (b * nb + jnp.maximum(i - 1, 0), 0))
    return pl.pallas_call(
        functools.partial(_swa_kernel, group=group),
        grid=(batch, nb),
        in_specs=[pl.BlockSpec(memory_space=pltpu.SMEM), cur(wq), prev(wk), cur(wk), prev(wk), cur(wk)],
        out_specs=cur(wq),
        out_shape=jax.ShapeDtypeStruct((n, wq), BF16),
        compiler_params=_cparams("parallel", "parallel"),
        name="swa",
    )(sinks.astype(F32), q, k, k, v, v)


RW_CHUNK = 64


def _mm(a, b):
    return jnp.dot(a.astype(BF16), b.astype(BF16), preferred_element_type=F32)


def _mm_nt(a, b):
    return lax.dot_general(a.astype(BF16), b.astype(BF16), (((1,), (1,)), ((), ())), preferred_element_type=F32)


def _mm_tn(a, b):
    return lax.dot_general(a.astype(BF16), b.astype(BF16), (((0,), (0,)), ((), ())), preferred_element_type=F32)


def _mm_f32(a, b):
    return jnp.dot(a, b, precision=lax.Precision.HIGHEST, preferred_element_type=F32)


def _rwkv_kernel(p_ref, mu_ref, w0_ref, w2_ref, a0_ref, a2_ref, g2_ref, kk_ref, ka_ref, rk_ref, lnw_ref, lnb_ref,
                 o_ref, last_ref, st_ref, *, width):
    c = pl.program_id(1)
    C = p_ref.shape[0]
    W = width
    N = RWKV_HEAD_DIM
    SL = 2 * N
    n_slab = W // SL

    @pl.when(c == 0)
    def _():
        last_ref[...] = jnp.zeros_like(last_ref)
        st_ref[...] = jnp.zeros_like(st_ref)

    p = p_ref[...]
    row = lax.broadcasted_iota(jnp.int32, (C, 1), 0)
    prev = jnp.where(row == 0, last_ref[...], pltpu.roll(p, 1, axis=0))
    last_ref[...] = p[C - 1:C, :]
    p = p + (prev - p) * mu_ref[...]
    r = p[:, :W]
    k = p[:, W:2 * W]
    v = p[:, 2 * W:3 * W]
    o3 = 3 * W
    dw = p[:, o3:o3 + DECAY_LORA]
    da = p[:, o3 + DECAY_LORA:o3 + DECAY_LORA + AAA_LORA]
    dg = p[:, o3 + DECAY_LORA + AAA_LORA:]

    z = -(w0_ref[...] + _mm(jnp.tanh(dw), w2_ref[...]))
    w_log = -(jnp.maximum(z, 0.0) + jnp.log(1.0 + jnp.exp(-jnp.abs(z)))) - 0.5
    lw = -jnp.exp(w_log)
    a = _sigmoid(a0_ref[...] + _mm(da, a2_ref[...]))
    g = _mm(_sigmoid(dg), g2_ref[...])

    ti = lax.broadcasted_iota(jnp.int32, (C, C), 0)
    tj = lax.broadcasted_iota(jnp.int32, (C, C), 1)
    cum = _mm_f32((ti >= tj).astype(F32), lw)
    e_pos = jnp.exp(cum)
    e_neg = jnp.exp(-cum)
    e_prev = jnp.exp(cum - lw)
    e_end = jnp.exp(cum[C - 1:C, :] - cum)
    p_end = e_pos[C - 1:C, :]

    li = lax.broadcasted_iota(jnp.int32, (SL, SL), 0)
    lj = lax.broadcasted_iota(jnp.int32, (SL, SL), 1)
    same = (li // N) == (lj // N)
    head_sum = same.astype(F32)
    strict = same & ((li % N) > (lj % N))
    incl = same & ((li % N) >= (lj % N))
    lane = lax.broadcasted_iota(jnp.int32, (C, SL), 1)
    m0 = lane < N

    def stack(t):
        return jnp.concatenate([jnp.where(m0, t, 0.0), jnp.where(m0, 0.0, t)], axis=0)

    for s in range(n_slab):
        sl = slice(s * SL, (s + 1) * SL)
        rs, ks, vs, as_ = r[:, sl], k[:, sl], v[:, sl], a[:, sl]
        x = ks * kk_ref[:, sl]
        ss = _mm_f32(x * x, head_sum)
        kk = x / jnp.maximum(jnp.sqrt(ss), 1e-12)
        k2 = ks * (1.0 + (as_ - 1.0) * ka_ref[:, sl])
        At = stack(-kk * e_prev[:, sl])
        Bt = stack(kk * as_ * e_neg[:, sl])
        Kt = stack(k2 * e_neg[:, sl])
        Rt = stack(rs * e_pos[:, sl])
        V2 = stack(vs)
        Be = stack(kk * as_ * e_end[:, sl])
        Ke = stack(k2 * e_end[:, sl])
        st = st_ref[s]

        BK = jnp.concatenate([Bt, Kt], axis=0)
        AR = jnp.concatenate([At, Rt], axis=0)
        G = _mm_nt(AR, BK)
        h2 = 2 * C
        A_ab = jnp.where(strict, G[:h2, :h2], 0.0)
        A_ak = jnp.where(strict, G[:h2, h2:], 0.0)
        A_rb = jnp.where(incl, G[h2:, :h2], 0.0)
        A_rk = jnp.where(incl, G[h2:, h2:], 0.0)
        ST = _mm(AR, st)
        U = ST[:h2] + _mm(A_ak, V2)
        Ap = A_ab
        steps = int(math.log2(C))
        for i in range(steps):
            U = U + _mm(Ap, U)
            if i + 1 < steps:
                Ap = _mm(Ap, Ap)
        O2 = ST[h2:] + _mm(A_rb, U) + _mm(A_rk, V2)
        st_new = st * jnp.transpose(jnp.broadcast_to(p_end[:, sl], (SL, SL))) + _mm_tn(Be, U) + _mm_tn(Ke, V2)
        st_ref[s] = jnp.where(same, st_new, 0.0)

        o = O2[:C] + O2[C:]
        mean = _mm_f32(o, head_sum) * (1.0 / N)
        dlt = o - mean
        var = _mm_f32(dlt * dlt, head_sum) * (1.0 / N)
        on = dlt * lax.rsqrt(var + RWKV_GN_EPS) * lnw_ref[:, sl] + lnb_ref[:, sl]
        bonus = _mm_f32(rs * k2 * rk_ref[:, sl], head_sum) * vs
        o_ref[:, sl] = ((on + bonus) * g[:, sl]).astype(o_ref.dtype)


def _rwkv(p, mu, w0, w2, a0, a2, g2, k_k, k_a, r_k, ln_w, ln_b, batch, seq):
    n, wp = p.shape
    width = w0.shape[-1]
    C = RW_CHUNK
    nc = seq // C
    vec = lambda t: t.reshape(1, -1).astype(F32)
    args = [vec(mu), vec(w0), w2.astype(BF16), vec(a0), a2.astype(BF16), g2.astype(BF16), vec(k_k), vec(k_a),
            vec(r_k), vec(ln_w), vec(ln_b)]
    return pl.pallas_call(
        functools.partial(_rwkv_kernel, width=width),
        grid=(batch, nc),
        in_specs=[pl.BlockSpec((C, wp), lambda b, c: (b * nc + c, 0))] + [_const_spec(t.shape) for t in args],
        out_specs=pl.BlockSpec((C, width), lambda b, c: (b * nc + c, 0)),
        out_shape=jax.ShapeDtypeStruct((n, width), BF16),
        scratch_shapes=[pltpu.VMEM((1, wp), F32),
                        pltpu.VMEM((width // (2 * RWKV_HEAD_DIM), 2 * RWKV_HEAD_DIM, 2 * RWKV_HEAD_DIM), F32)],
        compiler_params=_cparams("parallel", "arbitrary"),
        name="rwkv",
    )(p, *args)


def _memkv_kernel(m_ref, g_ref, w_ref, o_ref):
    o_ref[...] = jnp.dot(_rms(m_ref[...], g_ref[...]).astype(BF16), w_ref[...],
                         preferred_element_type=F32).astype(o_ref.dtype)


def _mem_kv(mem2, g_mem, w_kv_b, mem_len):
    n, d = mem2.shape
    wo = w_kv_b.shape[1]
    return pl.pallas_call(
        _memkv_kernel,
        grid=(n // mem_len,),
        in_specs=[pl.BlockSpec((mem_len, d), lambda i: (i, 0)), _const_spec((1, d)), _const_spec(w_kv_b.shape)],
        out_specs=pl.BlockSpec((mem_len, wo), lambda i: (i, 0)),
        out_shape=jax.ShapeDtypeStruct((n, wo), BF16),
        compiler_params=_cparams("parallel"),
        name="mem_kv",
    )(mem2, g_mem.reshape(1, d), w_kv_b)


def _post_kernel(x_ref, at_ref, rw_ref, gt_ref, kv_ref, wo_ref, gc_ref, wq_ref, woc_ref, gf_ref, wpq_ref,
                 k1_ref, k2_ref, h_ref, s1_ref, s2_ref):
    d = x_ref.shape[1]
    ga = gt_ref[:, :d].astype(F32)
    gb = gt_ref[:, d:].astype(F32)
    mixed = ga * at_ref[...].astype(F32) + gb * rw_ref[...].astype(F32)
    h1 = x_ref[...] + jnp.dot(mixed.astype(BF16), wo_ref[...], preferred_element_type=F32)

    qc = jnp.dot(_rms(h1, gc_ref[...]).astype(BF16), wq_ref[...], preferred_element_type=F32)
    wc = qc.shape[1]
    hd = wc // CROSS_HEADS
    scale = hd ** -0.5
    outs = []
    for hh in range(CROSS_HEADS):
        qh = (qc[:, hh * hd:(hh + 1) * hd] * scale).astype(BF16)
        kh = kv_ref[:, hh * hd:(hh + 1) * hd]
        vh = kv_ref[:, wc + hh * hd:wc + (hh + 1) * hd]
        s = lax.dot_general(qh, kh, (((1,), (1,)), ((), ())), preferred_element_type=F32)
        pr = jnp.exp(s - jnp.max(s, axis=-1, keepdims=True))
        pr = pr / jnp.sum(pr, axis=-1, keepdims=True)
        outs.append(jnp.dot(pr.astype(BF16), vh, preferred_element_type=F32))
    oc = jnp.concatenate(outs, axis=1)
    h2 = h1 + jnp.dot(oc.astype(BF16), woc_ref[...], preferred_element_type=F32)
    h_ref[...] = h2

    q3 = jnp.dot(_rms(h2, gf_ref[...]).astype(BF16), wpq_ref[...], preferred_element_type=F32)
    half = k1_ref.shape[1]
    for hh in range(PEER_HEADS):
        qa = q3[:, 2 * hh * half:(2 * hh + 1) * half]
        qb = q3[:, (2 * hh + 1) * half:(2 * hh + 2) * half]
        dn = (((1,), (1,)), ((), ()))
        s1_ref[hh] = lax.dot_general(k1_ref[...], qa, dn, precision=lax.Precision.HIGHEST, preferred_element_type=F32)
        s2_ref[hh] = lax.dot_general(k2_ref[...], qb, dn, precision=lax.Precision.HIGHEST, preferred_element_type=F32)


def _post(x2, attn, rw, gates, kv, w_out_b, g_cross, w_qc_b, w_oc_b, g_ffn, w_pq_b, k1, k2, batch, seq, mem_len, tm):
    n, d = x2.shape
    nt = seq // tm
    n_keys = k1.shape[0]
    row = lambda w: pl.BlockSpec((tm, w), lambda b, i: (b * nt + i, 0))
    sc_spec = pl.BlockSpec((PEER_HEADS, n_keys, tm), lambda b, i: (0, 0, b * nt + i))
    consts = [w_out_b, g_cross.reshape(1, d), w_qc_b, w_oc_b, g_ffn.reshape(1, d), w_pq_b, k1.astype(F32), k2.astype(F32)]
    return pl.pallas_call(
        _post_kernel,
        grid=(batch, nt),
        in_specs=[row(d), row(d), row(d), row(2 * d), pl.BlockSpec((mem_len, kv.shape[1]), lambda b, i: (b, 0))]
                 + [_const_spec(t.shape) for t in consts],
        out_specs=[row(d), sc_spec, sc_spec],
        out_shape=[jax.ShapeDtypeStruct((n, d), F32), jax.ShapeDtypeStruct((PEER_HEADS, n_keys, n), F32),
                   jax.ShapeDtypeStruct((PEER_HEADS, n_keys, n), F32)],
        compiler_params=_cparams("parallel", "parallel"),
        name="post",
    )(x2, attn, rw, gates, kv, *consts)


def _topk_rows(s, k):
    rows = s.shape[0]
    rid = lax.broadcasted_iota(jnp.int32, s.shape, 0)
    vals, idxs = [], []
    for _ in range(k):
        m = jnp.max(s, axis=0, keepdims=True)
        i = jnp.min(jnp.where(s == m, rid, rows), axis=0, keepdims=True)
        vals.append(m)
        idxs.append(i)
        s = jnp.where(rid == i, -jnp.inf, s)
    return jnp.concatenate(vals, axis=0), jnp.concatenate(idxs, axis=0)


def _pick_rows(table, sel, k):
    out = jnp.zeros_like(table)
    for a in range(k):
        out = jnp.where(sel == a, table[a:a + 1, :], out)
    return out


def _route_kernel(s1_ref, s2_ref, e_ref, g_ref):
    K = PEER_TOPK
    n_keys = s1_ref.shape[0]
    v1, i1 = _topk_rows(s1_ref[...], K)
    v2, i2 = _topk_rows(s2_ref[...], K)
    cand = jnp.concatenate([v1[a:a + 1, :] + v2 for a in range(K)], axis=0)
    sc, ci = _topk_rows(cand, K)
    e_ref[...] = _pick_rows(i1, ci // K, K) * n_keys + _pick_rows(i2, ci % K, K)
    ex = jnp.exp(sc - sc[0:1, :])
    g_ref[...] = ex / jnp.sum(ex, axis=0, keepdims=True)


def _route(s1, s2, tk):
    heads, n_keys, n = s1.shape
    spec_in = pl.BlockSpec((None, n_keys, tk), lambda i, h: (h, 0, i))
    spec_out = pl.BlockSpec((None, PEER_TOPK, tk), lambda i, h: (h, 0, i))
    return pl.pallas_call(
        _route_kernel,
        grid=(n // tk, heads),
        in_specs=[spec_in, spec_in],
        out_specs=[spec_out, spec_out],
        out_shape=[jax.ShapeDtypeStruct((heads, PEER_TOPK, n), jnp.int32),
                   jax.ShapeDtypeStruct((heads, PEER_TOPK, n), F32)],
        compiler_params=_cparams("parallel", "parallel"),
        name="route",
    )(s1, s2)


PEER_GROUP = 16
SUB = 8


def _gather_rows(tbl_ref, e_ref, t, g):
    parts = [tbl_ref[e_ref[t, g * PEER_GROUP + j]] for j in range(PEER_GROUP)]
    return pltpu.bitcast(jnp.concatenate(parts, axis=0), BF16)


def _diag_mask(shape):
    r = lax.broadcasted_iota(jnp.int32, shape, len(shape) - 2)
    c = lax.broadcasted_iota(jnp.int32, shape, len(shape) - 1)
    return r == (c % SUB)


def _split_dot(a, b01):
    hi = a.astype(BF16)
    lo = (a - hi.astype(F32)).astype(BF16)
    return jnp.dot(hi, b01, preferred_element_type=F32) + jnp.dot(lo, b01, preferred_element_type=F32)


def _peer_act_kernel(e_ref, h_ref, gf_ref, gate_ref, tbl_ref, sel_ref, o_ref, r_ref, xn_ref):
    tm = h_ref.shape[0]
    n_grp = e_ref.shape[1] // PEER_GROUP
    h = h_ref[...]
    ms = jnp.sum(jnp.sum(h * h, axis=2, keepdims=True), axis=1, keepdims=True) / (h.shape[1] * h.shape[2])
    xn_ref[...] = h * lax.rsqrt(ms + RMS_EPS) * gf_ref[...]

    def body(t, carry):
        xt = xn_ref[t].astype(BF16)
        for g in range(n_grp):
            w = _gather_rows(tbl_ref, e_ref, t, g)
            r_ref[t, :, g * LANES:(g + 1) * LANES] = lax.dot_general(
                xt, w, (((1,), (1,)), ((), ())), preferred_element_type=F32)
        return carry

    lax.fori_loop(0, tm, body, 0)
    r = r_ref[...]
    dots = jnp.sum(jnp.where(_diag_mask(r.shape), r, 0.0), axis=1)
    act = _split_dot(dots, sel_ref[...])
    gelu = 0.5 * act * (1.0 + lax.erf(act * (2.0 ** -0.5)))
    o_ref[...] = gate_ref[...] * gelu


def _peer_out_kernel(e_ref, h_ref, gw_ref, tbl_ref, exp_ref, gfin_ref, o_ref, ge_ref, acc_ref):
    tm = h_ref.shape[0]
    n_grp = e_ref.shape[1] // PEER_GROUP
    ge_ref[...] = _split_dot(gw_ref[...], exp_ref[...])
    mask = _diag_mask((SUB, n_grp * LANES))

    def body(t, carry):
        lhs = jnp.where(mask, jnp.broadcast_to(ge_ref[pl.ds(t, 1), :], mask.shape), 0.0).astype(BF16)
        acc = jnp.zeros((SUB, LANES), F32)
        for g in range(n_grp):
            w = _gather_rows(tbl_ref, e_ref, t, g)
            acc = acc + jnp.dot(lhs[:, g * LANES:(g + 1) * LANES], w, preferred_element_type=F32)
        acc_ref[t] = acc
        return carry

    lax.fori_loop(0, tm, body, 0)
    y = h_ref[...] + acc_ref[...]
    ms = jnp.sum(jnp.sum(y * y, axis=2, keepdims=True), axis=1, keepdims=True) / (y.shape[1] * y.shape[2])
    o_ref[...] = y * lax.rsqrt(ms + RMS_EPS) * gfin_ref[...]


def _table_tiles(t):
    e, d = t.shape
    p = d // (2 * LANES)
    tb = t.astype(BF16).reshape(e, p, 2, LANES).transpose(0, 1, 3, 2)
    return lax.bitcast_convert_type(tb, jnp.int32)


def _peer(h2, e_idx, gate, g_ffn, g_final, u_tiles, v_tiles, tm):
    n, d = h2.shape
    kk = e_idx.shape[1]
    h3 = h2.reshape(n, SUB, d // SUB)
    row3 = pl.BlockSpec((tm, SUB, d // SUB), lambda i: (i, 0, 0))
    row = pl.BlockSpec((tm, kk), lambda i: (i, 0))
    idx = pl.BlockSpec((tm, kk), lambda i: (i, 0), memory_space=pltpu.SMEM)
    lane = jnp.arange(kk * SUB) // SUB
    sel = (lane[:, None] == jnp.arange(kk)[None, :]).astype(BF16)
    gw = pl.pallas_call(
        _peer_act_kernel,
        grid=(n // tm,),
        in_specs=[idx, row3, _const_spec((1, SUB, d // SUB)), row, _const_spec(u_tiles.shape), _const_spec(sel.shape)],
        out_specs=row,
        out_shape=jax.ShapeDtypeStruct((n, kk), F32),
        scratch_shapes=[pltpu.VMEM((tm, SUB, kk * SUB), F32), pltpu.VMEM((tm, SUB, d // SUB), F32)],
        compiler_params=_cparams("arbitrary"),
        name="peer_act",
    )(e_idx, h3, g_ffn.reshape(1, SUB, d // SUB), gate, u_tiles, sel)
    out = pl.pallas_call(
        _peer_out_kernel,
        grid=(n // tm,),
        in_specs=[idx, row3, row, _const_spec(v_tiles.shape), _const_spec(sel.T.shape), _const_spec((1, SUB, d // SUB))],
        out_specs=row3,
        out_shape=jax.ShapeDtypeStruct((n, SUB, d // SUB), F32),
        scratch_shapes=[pltpu.VMEM((tm, kk * SUB), F32), pltpu.VMEM((tm, SUB, d // SUB), F32)],
        compiler_params=_cparams("arbitrary"),
        name="peer_out",
    )(e_idx, h3, gw, v_tiles, sel.T, g_final.reshape(1, SUB, d // SUB))
    return out.reshape(n, d)


def kernel(x, mem, positions, g_mix, w_in, attn_sinks, rwkv_mu, rwkv_w0, rwkv_w2, rwkv_a0, rwkv_a2, rwkv_g2, rwkv_k_k, rwkv_k_a, rwkv_r_k, rwkv_ln_w, rwkv_ln_b, w_out, g_cross, g_mem, w_q_cross, w_kv_cross, w_o_cross, g_ffn, peer_w_q, peer_sub_keys_1, peer_sub_keys_2, peer_u, peer_v, g_final):
    b, s, d = x.shape
    n = b * s
    L = 0
    wq = d
    wk = ATT_KV_HEADS * HEAD_DIM
    wp = 3 * d + DECAY_LORA + AAA_LORA + GATE_LORA
    widths = (wq, wk, wk, wp, 2 * d)
    rc, rs = _rope_tables(positions)
    q, k, v, p, gates = _in_proj(x.reshape(n, d), g_mix[L], w_in[L].astype(BF16), rc, rs, widths, 256)
    attn = _swa(q, k, v, attn_sinks[L], b, s)
    rw = _rwkv(p, rwkv_mu[L], rwkv_w0[L], rwkv_w2[L], rwkv_a0[L], rwkv_a2[L], rwkv_g2[L], rwkv_k_k[L], rwkv_k_a[L],
               rwkv_r_k[L], rwkv_ln_w[L], rwkv_ln_b[L], b, s)
    m = mem.shape[1]
    kv = _mem_kv(mem.reshape(b * m, d), g_mem[L], w_kv_cross[L].astype(BF16), m)
    h2, s1, s2 = _post(x.reshape(n, d), attn, rw, gates, kv, w_out[L].astype(BF16), g_cross[L],
                       w_q_cross[L].astype(BF16), w_o_cross[L].astype(BF16), g_ffn[L], peer_w_q[L].astype(BF16),
                       peer_sub_keys_1[L], peer_sub_keys_2[L], b, s, m, 256)
    e_t, g_t = _route(s1, s2, LANES)
    kk = PEER_HEADS * PEER_TOPK
    e_idx = e_t.reshape(kk, n).T
    gate = g_t.reshape(kk, n).T
    out = _peer(h2, e_idx, gate, g_ffn[L], g_final, _table_tiles(peer_u[L]), _table_tiles(peer_v[L]), 64)
    return out.reshape(b, s, d)
```

```python
import functools
import math

import jax
import jax.numpy as jnp
from jax import lax
from jax.experimental import pallas as pl
from jax.experimental.pallas import tpu as pltpu

F32 = jnp.float32
BF16 = jnp.bfloat16

RMS_EPS = 1e-5
HEAD_DIM = 64
ATT_KV_HEADS = 4
WINDOW = 128
ROT_DIM = HEAD_DIM // 4
ROPE_THETA = 500000.0
RWKV_HEAD_DIM = 64
DECAY_LORA = 64
AAA_LORA = 64
GATE_LORA = 128
RWKV_GN_EPS = 64e-5
CROSS_HEADS = 4
PEER_HEADS = 8
PEER_TOPK = 16

LANES = 128
VMEM_LIMIT = 56 * 1024 * 1024


def _cparams(*sem):
    return pltpu.CompilerParams(dimension_semantics=sem, vmem_limit_bytes=VMEM_LIMIT)


def _const_spec(shape):
    nd = len(shape)
    return pl.BlockSpec(shape, lambda *_: (0,) * nd, pipeline_mode=pl.Buffered(1))


def _rms(x, g):
    return x * lax.rsqrt(jnp.mean(x * x, axis=-1, keepdims=True) + RMS_EPS) * g


def _sigmoid(x):
    return 1.0 / (1.0 + jnp.exp(-x))


def _rope(t, c, s):
    w = t.shape[-1]
    rep = w // LANES
    cc = jnp.tile(c, (1, rep))
    ss = jnp.tile(s, (1, rep))
    lane = lax.broadcasted_iota(jnp.int32, t.shape, 1)
    first = (lane % HEAD_DIM) < (ROT_DIM // 2)
    partner = jnp.where(first, pltpu.roll(t, w - ROT_DIM // 2, axis=1), pltpu.roll(t, ROT_DIM // 2, axis=1))
    return t * cc + partner * ss


def _inproj_kernel(x_ref, g_ref, w_ref, c_ref, s_ref, q_ref, k_ref, v_ref, p_ref, gate_ref, *, widths):
    wq, wk, wv, wp, wg = widths
    xb = _rms(x_ref[...], g_ref[...]).astype(BF16)
    c = c_ref[...]
    s = s_ref[...]
    o = 0
    q = jnp.dot(xb, w_ref[:, o:o + wq], preferred_element_type=F32)
    q_ref[...] = _rope(q, c, s).astype(q_ref.dtype)
    o += wq
    k = jnp.dot(xb, w_ref[:, o:o + wk], preferred_element_type=F32)
    k_ref[...] = _rope(k, c, s).astype(k_ref.dtype)
    o += wk
    v_ref[...] = jnp.dot(xb, w_ref[:, o:o + wv], preferred_element_type=F32).astype(v_ref.dtype)
    o += wv
    p_ref[...] = jnp.dot(xb, w_ref[:, o:o + wp], preferred_element_type=F32)
    o += wp
    gate_ref[...] = _sigmoid(jnp.dot(xb, w_ref[:, o:o + wg], preferred_element_type=F32)).astype(gate_ref.dtype)


def _in_proj(x2, g_mix, w_in_b, rope_c, rope_s, widths, tm):
    n, d = x2.shape
    wq, wk, wv, wp, wg = widths
    row = lambda w: pl.BlockSpec((tm, w), lambda i: (i, 0))
    return pl.pallas_call(
        functools.partial(_inproj_kernel, widths=widths),
        grid=(n // tm,),
        in_specs=[row(d), _const_spec((1, d)), _const_spec(w_in_b.shape), row(LANES), row(LANES)],
        out_specs=[row(wq), row(wk), row(wv), row(wp), row(wg)],
        out_shape=[jax.ShapeDtypeStruct((n, wq), BF16), jax.ShapeDtypeStruct((n, wk), BF16),
                   jax.ShapeDtypeStruct((n, wv), BF16), jax.ShapeDtypeStruct((n, wp), F32),
                   jax.ShapeDtypeStruct((n, wg), BF16)],
        compiler_params=_cparams("parallel"),
        name="in_proj",
    )(x2, g_mix.reshape(1, d), w_in_b, rope_c, rope_s)


def _rope_tables(positions):
    half = ROT_DIM // 2
    inv_freq = 1.0 / (ROPE_THETA ** (jnp.arange(0, ROT_DIM, 2, dtype=F32) / ROT_DIM))
    ang = positions.reshape(-1).astype(F32)[:, None] * inv_freq
    cos, sin = jnp.cos(ang), jnp.sin(ang)
    n = ang.shape[0]
    pad1 = jnp.ones((n, HEAD_DIM - ROT_DIM), F32)
    pad0 = jnp.zeros((n, HEAD_DIM - ROT_DIM), F32)
    c = jnp.concatenate([cos, cos, pad1], axis=1)
    s = jnp.concatenate([-sin, sin, pad0], axis=1)
    return jnp.tile(c, (1, LANES // HEAD_DIM)), jnp.tile(s, (1, LANES // HEAD_DIM))


def _swa_kernel(sink_ref, q_ref, kp_ref, kc_ref, vp_ref, vc_ref, o_ref, *, group):
    n = pl.program_id(1)
    blk = q_ref.shape[0]
    scale = HEAD_DIM ** -0.5
    rows = group * blk
    qi = lax.broadcasted_iota(jnp.int32, (rows, 2 * blk), 0) % blk
    kj = lax.broadcasted_iota(jnp.int32, (rows, 2 * blk), 1)
    valid = (kj > qi) & (kj <= qi + WINDOW) & ((kj >= blk) | (n > 0))
    rid = lax.broadcasted_iota(jnp.int32, (rows, 1), 0) // blk
    for kv in range(ATT_KV_HEADS):
        sl = slice(kv * HEAD_DIM, (kv + 1) * HEAD_DIM)
        kk = jnp.concatenate([kp_ref[:, sl], kc_ref[:, sl]], axis=0)
        vv = jnp.concatenate([vp_ref[:, sl], vc_ref[:, sl]], axis=0)
        heads = [kv * group + g for g in range(group)]
        qq = jnp.concatenate([q_ref[:, h * HEAD_DIM:(h + 1) * HEAD_DIM] for h in heads], axis=0)
        s = lax.dot_general(qq, kk, (((1,), (1,)), ((), ())), preferred_element_type=F32) * scale
        s = jnp.where(valid, s, -1e30)
        sink = jnp.zeros((rows, 1), F32)
        for g, h in enumerate(heads):
            sink = jnp.where(rid == g, sink_ref[h], sink)
        m = jnp.maximum(jnp.max(s, axis=-1, keepdims=True), sink)
        p = jnp.exp(s - m)
        denom = jnp.sum(p, axis=-1, keepdims=True) + jnp.exp(sink - m)
        o = jnp.dot((p / denom).astype(vv.dtype), vv, preferred_element_type=F32)
        for g, h in enumerate(heads):
            o_ref[:, h * HEAD_DIM:(h + 1) * HEAD_DIM] = o[g * blk:(g + 1) * blk].astype(o_ref.dtype)


def _swa(q, k, v, sinks, batch, seq):
    n, wq = q.shape
    wk = k.shape[1]
    blk = WINDOW
    nb = seq // blk
    group = (wq // HEAD_DIM) // ATT_KV_HEADS
    cur = lambda w: pl.BlockSpec((blk, w), lambda b, i: (b * nb + i, 0))
    prev = lambda w: pl.BlockSpec((blk, w), lambda b, i: (b * nb + jnp.maximum(i - 1, 0), 0))
    return pl.pallas_call(
        functools.partial(_swa_kernel, group=group),
        grid=(batch, nb),
        in_specs=[pl.BlockSpec(memory_space=pltpu.SMEM), cur(wq), prev(wk), cur(wk), prev(wk), cur(wk)],
        out_specs=cur(wq),
        out_shape=jax.ShapeDtypeStruct((n, wq), BF16),
        compiler_params=_cparams("parallel", "parallel"),
        name="swa",
    )(sinks.astype(F32), q, k, k, v, v)


RW_CHUNK = 64


def _mm(a, b):
    return jnp.dot(a.astype(BF16), b.astype(BF16), preferred_element_type=F32)


def _mm_nt(a, b):
    return lax.dot_general(a.astype(BF16), b.astype(BF16), (((1,), (1,)), ((), ())), preferred_element_type=F32)


def _mm_tn(a, b):
    return lax.dot_general(a.astype(BF16), b.astype(BF16), (((0,), (0,)), ((), ())), preferred_element_type=F32)


def _mm_f32(a, b):
    return jnp.dot(a, b, precision=lax.Precision.HIGHEST, preferred_element_type=F32)


def _rwkv_kernel(p_ref, mu_ref, w0_ref, w2_ref, a0_ref, a2_ref, g2_ref, kk_ref, ka_ref, rk_ref, lnw_ref, lnb_ref,
                 o_ref, last_ref, st_ref, *, width):
    c = pl.program_id(1)
    C = p_ref.shape[0]
    W = width
    N = RWKV_HEAD_DIM
    SL = 2 * N
    n_slab = W // SL

    @pl.when(c == 0)
    def _():
        last_ref[...] = jnp.zeros_like(last_ref)
        st_ref[...] = jnp.zeros_like(st_ref)

    p = p_ref[...]
    row = lax.broadcasted_iota(jnp.int32, (C, 1), 0)
    prev = jnp.where(row == 0, last_ref[...], pltpu.roll(p, 1, axis=0))
    last_ref[...] = p[C - 1:C, :]
    p = p + (prev - p) * mu_ref[...]
    r = p[:, :W]
    k = p[:, W:2 * W]
    v = p[:, 2 * W:3 * W]
    o3 = 3 * W
    dw = p[:, o3:o3 + DECAY_LORA]
    da = p[:, o3 + DECAY_LORA:o3 + DECAY_LORA + AAA_LORA]
    dg = p[:, o3 + DECAY_LORA + AAA_LORA:]

    z = -(w0_ref[...] + _mm(jnp.tanh(dw), w2_ref[...]))
    w_log = -(jnp.maximum(z, 0.0) + jnp.log(1.0 + jnp.exp(-jnp.abs(z)))) - 0.5
    lw = -jnp.exp(w_log)
    a = _sigmoid(a0_ref[...] + _mm(da, a2_ref[...]))
    g = _mm(_sigmoid(dg), g2_ref[...])

    ti = lax.broadcasted_iota(jnp.int32, (C, C), 0)
    tj = lax.broadcasted_iota(jnp.int32, (C, C), 1)
    cum = _mm_f32((ti >= tj).astype(F32), lw)
    e_pos = jnp.exp(cum)
    e_neg = jnp.exp(-cum)
    e_prev = jnp.exp(cum - lw)
    e_end = jnp.exp(cum[C - 1:C, :] - cum)
    p_end = e_pos[C - 1:C, :]

    li = lax.broadcasted_iota(jnp.int32, (SL, SL), 0)
    lj = lax.broadcasted_iota(jnp.int32, (SL, SL), 1)
    same = (li // N) == (lj // N)
    head_sum = same.astype(F32)
    strict = same & ((li % N) > (lj % N))
    incl = same & ((li % N) >= (lj % N))
    lane = lax.broadcasted_iota(jnp.int32, (C, SL), 1)
    m0 = lane < N

    def stack(t):
        return jnp.concatenate([jnp.where(m0, t, 0.0), jnp.where(m0, 0.0, t)], axis=0)

    for s in range(n_slab):
        sl = slice(s * SL, (s + 1) * SL)
        rs, ks, vs, as_ = r[:, sl], k[:, sl], v[:, sl], a[:, sl]
        x = ks * kk_ref[:, sl]
        ss = _mm_f32(x * x, head_sum)
        kk = x / jnp.maximum(jnp.sqrt(ss), 1e-12)
        k2 = ks * (1.0 + (as_ - 1.0) * ka_ref[:, sl])
        At = stack(-kk * e_prev[:, sl])
        Bt = stack(kk * as_ * e_neg[:, sl])
        Kt = stack(k2 * e_neg[:, sl])
        Rt = stack(rs * e_pos[:, sl])
        V2 = stack(vs)
        Be = stack(kk * as_ * e_end[:, sl])
        Ke = stack(k2 * e_end[:, sl])
        st = st_ref[s]

        BK = jnp.concatenate([Bt, Kt], axis=0)
        AR = jnp.concatenate([At, Rt], axis=0)
        G = _mm_nt(AR, BK)
        h2 = 2 * C
        A_ab = jnp.where(strict, G[:h2, :h2], 0.0)
        A_ak = jnp.where(strict, G[:h2, h2:], 0.0)
        A_rb = jnp.where(incl, G[h2:, :h2], 0.0)
        A_rk = jnp.where(incl, G[h2:, h2:], 0.0)
        ST = _mm(AR, st)
        U = ST[:h2] + _mm(A_ak, V2)
        Ap = A_ab
        steps = int(math.log2(C))
        for i in range(steps):
            U = U + _mm(Ap, U)
            if i + 1 < steps:
                Ap = _mm(Ap, Ap)
        O2 = ST[h2:] + _mm(A_rb, U) + _mm(A_rk, V2)
        st_new = st * jnp.transpose(jnp.broadcast_to(p_end[:, sl], (SL, SL))) + _mm_tn(Be, U) + _mm_tn(Ke, V2)
        st_ref[s] = jnp.where(same, st_new, 0.0)

        o = O2[:C] + O2[C:]
        mean = _mm_f32(o, head_sum) * (1.0 / N)
        dlt = o - mean
        var = _mm_f32(dlt * dlt, head_sum) * (1.0 / N)
        on = dlt * lax.rsqrt(var + RWKV_GN_EPS) * lnw_ref[:, sl] + lnb_ref[:, sl]
        bonus = _mm_f32(rs * k2 * rk_ref[:, sl], head_sum) * vs
        o_ref[:, sl] = ((on + bonus) * g[:, sl]).astype(o_ref.dtype)


def _rwkv(p, mu, w0, w2, a0, a2, g2, k_k, k_a, r_k, ln_w, ln_b, batch, seq):
    n, wp = p.shape
    width = w0.shape[-1]
    C = RW_CHUNK
    nc = seq // C
    vec = lambda t: t.reshape(1, -1).astype(F32)
    args = [vec(mu), vec(w0), w2.astype(BF16), vec(a0), a2.astype(BF16), g2.astype(BF16), vec(k_k), vec(k_a),
            vec(r_k), vec(ln_w), vec(ln_b)]
    return pl.pallas_call(
        functools.partial(_rwkv_kernel, width=width),
        grid=(batch, nc),
        in_specs=[pl.BlockSpec((C, wp), lambda b, c: (b * nc + c, 0))] + [_const_spec(t.shape) for t in args],
        out_specs=pl.BlockSpec((C, width), lambda b, c: (b * nc + c, 0)),
        out_shape=jax.ShapeDtypeStruct((n, width), BF16),
        scratch_shapes=[pltpu.VMEM((1, wp), F32),
                        pltpu.VMEM((width // (2 * RWKV_HEAD_DIM), 2 * RWKV_HEAD_DIM, 2 * RWKV_HEAD_DIM), F32)],
        compiler_params=_cparams("parallel", "arbitrary"),
        name="rwkv",
    )(p, *args)


def _memkv_kernel(m_ref, g_ref, w_ref, o_ref):
    o_ref[...] = jnp.dot(_rms(m_ref[...], g_ref[...]).astype(BF16), w_ref[...],
                         preferred_element_type=F32).astype(o_ref.dtype)


def _mem_kv(mem2, g_mem, w_kv_b, mem_len):
    n, d = mem2.shape
    wo = w_kv_b.shape[1]
    return pl.pallas_call(
        _memkv_kernel,
        grid=(n // mem_len,),
        in_specs=[pl.BlockSpec((mem_len, d), lambda i: (i, 0)), _const_spec((1, d)), _const_spec(w_kv_b.shape)],
        out_specs=pl.BlockSpec((mem_len, wo), lambda i: (i, 0)),
        out_shape=jax.ShapeDtypeStruct((n, wo), BF16),
        compiler_params=_cparams("parallel"),
        name="mem_kv",
    )(mem2, g_mem.reshape(1, d), w_kv_b)


def _post_kernel(x_ref, at_ref, rw_ref, gt_ref, kv_ref, wo_ref, gc_ref, wq_ref, woc_ref, gf_ref, wpq_ref,
                 k1_ref, k2_ref, h_ref, s1_ref, s2_ref):
    d = x_ref.shape[1]
    ga = gt_ref[:, :d].astype(F32)
    gb = gt_ref[:, d:].astype(F32)
    mixed = ga * at_ref[...].astype(F32) + gb * rw_ref[...].astype(F32)
    h1 = x_ref[...] + jnp.dot(mixed.astype(BF16), wo_ref[...], preferred_element_type=F32)

    qc = jnp.dot(_rms(h1, gc_ref[...]).astype(BF16), wq_ref[...], preferred_element_type=F32)
    wc = qc.shape[1]
    hd = wc // CROSS_HEADS
    scale = hd ** -0.5
    outs = []
    for hh in range(CROSS_HEADS):
        qh = (qc[:, hh * hd:(hh + 1) * hd] * scale).astype(BF16)
        kh = kv_ref[:, hh * hd:(hh + 1) * hd]
        vh = kv_ref[:, wc + hh * hd:wc + (hh + 1) * hd]
        s = lax.dot_general(qh, kh, (((1,), (1,)), ((), ())), preferred_element_type=F32)
        pr = jnp.exp(s - jnp.max(s, axis=-1, keepdims=True))
        pr = pr / jnp.sum(pr, axis=-1, keepdims=True)
        outs.append(jnp.dot(pr.astype(BF16), vh, preferred_element_type=F32))
    oc = jnp.concatenate(outs, axis=1)
    h2 = h1 + jnp.dot(oc.astype(BF16), woc_ref[...], preferred_element_type=F32)
    h_ref[...] = h2

    q3 = jnp.dot(_rms(h2, gf_ref[...]).astype(BF16), wpq_ref[...], preferred_element_type=F32)
    half = k1_ref.shape[1]
    for hh in range(PEER_HEADS):
        qa = q3[:, 2 * hh * half:(2 * hh + 1) * half]
        qb = q3[:, (2 * hh + 1) * half:(2 * hh + 2) * half]
        dn = (((1,), (1,)), ((), ()))
        s1_ref[hh] = lax.dot_general(k1_ref[...], qa, dn, precision=lax.Precision.HIGHEST, preferred_element_type=F32)
        s2_ref[hh] = lax.dot_general(k2_ref[...], qb, dn, precision=lax.Precision.HIGHEST, preferred_element_type=F32)


def _post(x2, attn, rw, gates, kv, w_out_b, g_cross, w_qc_b, w_oc_b, g_ffn, w_pq_b, k1, k2, batch, seq, mem_len, tm):
    n, d = x2.shape
    nt = seq // tm
    n_keys = k1.shape[0]
    row = lambda w: pl.BlockSpec((tm, w), lambda b, i: (b * nt + i, 0))
    sc_spec = pl.BlockSpec((PEER_HEADS, n_keys, tm), lambda b, i: (0, 0, b * nt + i))
    consts = [w_out_b, g_cross.reshape(1, d), w_qc_b, w_oc_b, g_ffn.reshape(1, d), w_pq_b, k1.astype(F32), k2.astype(F32)]
    return pl.pallas_call(
        _post_kernel,
        grid=(batch, nt),
        in_specs=[row(d), row(d), row(d), row(2 * d), pl.BlockSpec((mem_len, kv.shape[1]), lambda b, i: (b, 0))]
                 + [_const_spec(t.shape) for t in consts],
        out_specs=[row(d), sc_spec, sc_spec],
        out_shape=[jax.ShapeDtypeStruct((n, d), F32), jax.ShapeDtypeStruct((PEER_HEADS, n_keys, n), F32),
                   jax.ShapeDtypeStruct((PEER_HEADS, n_keys, n), F32)],
        compiler_params=_cparams("parallel", "parallel"),
        name="post",
    )(x2, attn, rw, gates, kv, *consts)


def _topk_rows(s, k):
    rows = s.shape[0]
    rid = lax.broadcasted_iota(jnp.int32, s.shape, 0)
    vals, idxs = [], []
    for _ in range(k):
        m = jnp.max(s, axis=0, keepdims=True)
        i = jnp.min(jnp.where(s == m, rid, rows), axis=0, keepdims=True)
        vals.append(m)
        idxs.append(i)
        s = jnp.where(rid == i, -jnp.inf, s)
    return jnp.concatenate(vals, axis=0), jnp.concatenate(idxs, axis=0)


def _pick_rows(table, sel, k):
    out = jnp.zeros_like(table)
    for a in range(k):
        out = jnp.where(sel == a, table[a:a + 1, :], out)
    return out


def _route_kernel(s1_ref, s2_ref, e_ref, g_ref):
    K = PEER_TOPK
    n_keys = s1_ref.shape[0]
    v1, i1 = _topk_rows(s1_ref[...], K)
    v2, i2 = _topk_rows(s2_ref[...], K)
    cand = jnp.concatenate([v1[a:a + 1, :] + v2 for a in range(K)], axis=0)
    sc, ci = _topk_rows(cand, K)
    e_ref[...] = _pick_rows(i1, ci // K, K) * n_keys + _pick_rows(i2, ci % K, K)
    ex = jnp.exp(sc - sc[0:1, :])
    g_ref[...] = ex / jnp.sum(ex, axis=0, keepdims=True)


def _route(s1, s2, tk):
    heads, n_keys, n = s1.shape
    spec_in = pl.BlockSpec((None, n_keys, tk), lambda i, h: (h, 0, i))
    spec_out = pl.BlockSpec((None, PEER_TOPK, tk), lambda i, h: (h, 0, i))
    return pl.pallas_call(
        _route_kernel,
        grid=(n // tk, heads),
        in_specs=[spec_in, spec_in],
        out_specs=[spec_out, spec_out],
        out_shape=[jax.ShapeDtypeStruct((heads, PEER_TOPK, n), jnp.int32),
                   jax.ShapeDtypeStruct((heads, PEER_TOPK, n), F32)],
        compiler_params=_cparams("parallel", "parallel"),
        name="route",
    )(s1, s2)


PEER_GROUP = 16
SUB = 8
PEER_UNROLL = 4


def _gather_rows(tbl_ref, e_ref, t, g):
    parts = [tbl_ref[pl.ds(pl.multiple_of(e_ref[t, g * PEER_GROUP + j], 4), 4), :] for j in range(PEER_GROUP)]
    return pltpu.bitcast(jnp.concatenate(parts, axis=0), BF16)


def _diag_mask(shape):
    r = lax.broadcasted_iota(jnp.int32, shape, len(shape) - 2)
    c = lax.broadcasted_iota(jnp.int32, shape, len(shape) - 1)
    return r == (c % SUB)


def _split_dot(a, b01):
    hi = a.astype(BF16)
    lo = (a - hi.astype(F32)).astype(BF16)
    return jnp.dot(hi, b01, preferred_element_type=F32) + jnp.dot(lo, b01, preferred_element_type=F32)


def _peer_act_kernel(e_ref, h_ref, gf_ref, gate_ref, tbl_ref, sel_ref, o_ref, r_ref, xn_ref):
    tm = h_ref.shape[0]
    n_grp = e_ref.shape[1] // PEER_GROUP
    h = h_ref[...]
    ms = jnp.sum(jnp.sum(h * h, axis=2, keepdims=True), axis=1, keepdims=True) / (h.shape[1] * h.shape[2])
    xn_ref[...] = h * lax.rsqrt(ms + RMS_EPS) * gf_ref[...]

    def body(i, carry):
        for j in range(PEER_UNROLL):
            t = i * PEER_UNROLL + j
            xt = xn_ref[t].astype(BF16)
            for g in range(n_grp):
                w = _gather_rows(tbl_ref, e_ref, t, g)
                r_ref[t, :, g * LANES:(g + 1) * LANES] = lax.dot_general(
                    xt, w, (((1,), (1,)), ((), ())), preferred_element_type=F32)
        return carry

    lax.fori_loop(0, tm // PEER_UNROLL, body, 0)
    r = r_ref[...]
    dots = jnp.sum(jnp.where(_diag_mask(r.shape), r, 0.0), axis=1)
    act = _split_dot(dots, sel_ref[...])
    gelu = 0.5 * act * (1.0 + lax.erf(act * (2.0 ** -0.5)))
    o_ref[...] = gate_ref[...] * gelu


def _peer_out_kernel(e_ref, h_ref, gw_ref, tbl_ref, exp_ref, gfin_ref, o_ref, ge_ref, acc_ref):
    tm = h_ref.shape[0]
    n_grp = e_ref.shape[1] // PEER_GROUP
    ge_ref[...] = _split_dot(gw_ref[...], exp_ref[...])
    mask = _diag_mask((SUB, n_grp * LANES))

    def body(i, carry):
        for j in range(PEER_UNROLL):
            t = i * PEER_UNROLL + j
            lhs = jnp.where(mask, jnp.broadcast_to(ge_ref[pl.ds(t, 1), :], mask.shape), 0.0).astype(BF16)
            acc = jnp.zeros((SUB, LANES), F32)
            for g in range(n_grp):
                w = _gather_rows(tbl_ref, e_ref, t, g)
                acc = acc + jnp.dot(lhs[:, g * LANES:(g + 1) * LANES], w, preferred_element_type=F32)
            acc_ref[t] = acc
        return carry

    lax.fori_loop(0, tm // PEER_UNROLL, body, 0)
    y = h_ref[...] + acc_ref[...]
    ms = jnp.sum(jnp.sum(y * y, axis=2, keepdims=True), axis=1, keepdims=True) / (y.shape[1] * y.shape[2])
    o_ref[...] = y * lax.rsqrt(ms + RMS_EPS) * gfin_ref[...]


def _table_tiles(t):
    e, d = t.shape
    p = d // (2 * LANES)
    tb = t.astype(BF16).reshape(e, p, 2, LANES).transpose(0, 1, 3, 2)
    return lax.bitcast_convert_type(tb, jnp.int32).reshape(e * p, LANES)


def _peer(h2, e_idx, gate, g_ffn, g_final, u_tiles, v_tiles, tm):
    n, d = h2.shape
    kk = e_idx.shape[1]
    h3 = h2.reshape(n, SUB, d // SUB)
    row3 = pl.BlockSpec((tm, SUB, d // SUB), lambda i: (i, 0, 0))
    row = pl.BlockSpec((tm, kk), lambda i: (i, 0))
    idx = pl.BlockSpec((tm, kk), lambda i: (i, 0), memory_space=pltpu.SMEM)
    lane = jnp.arange(kk * SUB) // SUB
    sel = (lane[:, None] == jnp.arange(kk)[None, :]).astype(BF16)
    gw = pl.pallas_call(
        _peer_act_kernel,
        grid=(n // tm,),
        in_specs=[idx, row3, _const_spec((1, SUB, d // SUB)), row, _const_spec(u_tiles.shape), _const_spec(sel.shape)],
        out_specs=row,
        out_shape=jax.ShapeDtypeStruct((n, kk), F32),
        scratch_shapes=[pltpu.VMEM((tm, SUB, kk * SUB), F32), pltpu.VMEM((tm, SUB, d // SUB), F32)],
        compiler_params=_cparams("arbitrary"),
        name="peer_act",
    )(e_idx, h3, g_ffn.reshape(1, SUB, d // SUB), gate, u_tiles, sel)
    out = pl.pallas_call(
        _peer_out_kernel,
        grid=(n // tm,),
        in_specs=[idx, row3, row, _const_spec(v_tiles.shape), _const_spec(sel.T.shape), _const_spec((1, SUB, d // SUB))],
        out_specs=row3,
        out_shape=jax.ShapeDtypeStruct((n, SUB, d // SUB), F32),
        scratch_shapes=[pltpu.VMEM((tm, kk * SUB), F32), pltpu.VMEM((tm, SUB, d // SUB), F32)],
        compiler_params=_cparams("arbitrary"),
        name="peer_out",
    )(e_idx, h3, gw, v_tiles, sel.T, g_final.reshape(1, SUB, d // SUB))
    return out.reshape(n, d)


def kernel(x, mem, positions, g_mix, w_in, attn_sinks, rwkv_mu, rwkv_w0, rwkv_w2, rwkv_a0, rwkv_a2, rwkv_g2, rwkv_k_k, rwkv_k_a, rwkv_r_k, rwkv_ln_w, rwkv_ln_b, w_out, g_cross, g_mem, w_q_cross, w_kv_cross, w_o_cross, g_ffn, peer_w_q, peer_sub_keys_1, peer_sub_keys_2, peer_u, peer_v, g_final):
    b, s, d = x.shape
    n = b * s
    L = 0
    wq = d
    wk = ATT_KV_HEADS * HEAD_DIM
    wp = 3 * d + DECAY_LORA + AAA_LORA + GATE_LORA
    widths = (wq, wk, wk, wp, 2 * d)
    rc, rs = _rope_tables(positions)
    q, k, v, p, gates = _in_proj(x.reshape(n, d), g_mix[L], w_in[L].astype(BF16), rc, rs, widths, 256)
    attn = _swa(q, k, v, attn_sinks[L], b, s)
    rw = _rwkv(p, rwkv_mu[L], rwkv_w0[L], rwkv_w2[L], rwkv_a0[L], rwkv_a2[L], rwkv_g2[L], rwkv_k_k[L], rwkv_k_a[L],
               rwkv_r_k[L], rwkv_ln_w[L], rwkv_ln_b[L], b, s)
    m = mem.shape[1]
    kv = _mem_kv(mem.reshape(b * m, d), g_mem[L], w_kv_cross[L].astype(BF16), m)
    h2, s1, s2 = _post(x.reshape(n, d), attn, rw, gates, kv, w_out[L].astype(BF16), g_cross[L],
                       w_q_cross[L].astype(BF16), w_o_cross[L].astype(BF16), g_ffn[L], peer_w_q[L].astype(BF16),
                       peer_sub_keys_1[L], peer_sub_keys_2[L], b, s, m, 256)
    e_t, g_t = _route(s1, s2, LANES)
    kk = PEER_HEADS * PEER_TOPK
    e_idx = e_t.reshape(kk, n).T * 4
    gate = g_t.reshape(kk, n).T
    out = _peer(h2, e_idx, gate, g_ffn[L], g_final, _table_tiles(peer_u[L]), _table_tiles(peer_v[L]), 64)
    return out.reshape(b, s, d)
```

```python
import functools
import math

import jax
import jax.numpy as jnp
from jax import lax
from jax.experimental import pallas as pl
from jax.experimental.pallas import tpu as pltpu

F32 = jnp.float32
BF16 = jnp.bfloat16

RMS_EPS = 1e-5
HEAD_DIM = 64
ATT_KV_HEADS = 4
WINDOW = 128
ROT_DIM = HEAD_DIM // 4
ROPE_THETA = 500000.0
RWKV_HEAD_DIM = 64
DECAY_LORA = 64
AAA_LORA = 64
GATE_LORA = 128
RWKV_GN_EPS = 64e-5
CROSS_HEADS = 4
PEER_HEADS = 8
PEER_TOPK = 16

LANES = 128
SUBLANES = 8
VMEM_LIMIT = 56 * 1024 * 1024


def _cparams(*sem):
    return pltpu.CompilerParams(dimension_semantics=sem, vmem_limit_bytes=VMEM_LIMIT)


def _const_spec(shape):
    nd = len(shape)
    return pl.BlockSpec(shape, lambda *_: (0,) * nd, pipeline_mode=pl.Buffered(1))


def _rms(x, g):
    return x * lax.rsqrt(jnp.mean(x * x, axis=-1, keepdims=True) + RMS_EPS) * g


def _sigmoid(x):
    return 1.0 / (1.0 + jnp.exp(-x))


def _rope(t, c, s):
    w = t.shape[-1]
    rep = w // LANES
    cc = jnp.tile(c, (1, rep))
    ss = jnp.tile(s, (1, rep))
    lane = lax.broadcasted_iota(jnp.int32, t.shape, 1)
    first = (lane % HEAD_DIM) < (ROT_DIM // 2)
    partner = jnp.where(first, pltpu.roll(t, w - ROT_DIM // 2, axis=1), pltpu.roll(t, ROT_DIM // 2, axis=1))
    return t * cc + partner * ss


def _inproj_kernel(x_ref, g_ref, w_ref, c_ref, s_ref, q_ref, k_ref, v_ref, p_ref, gate_ref, *, widths):
    wq, wk, wv, wp, wg = widths
    xb = _rms(x_ref[...], g_ref[...]).astype(BF16)
    c = c_ref[...]
    s = s_ref[...]
    o = 0
    q = jnp.dot(xb, w_ref[:, o:o + wq], preferred_element_type=F32)
    q_ref[...] = _rope(q, c, s).astype(q_ref.dtype)
    o += wq
    k = jnp.dot(xb, w_ref[:, o:o + wk], preferred_element_type=F32)
    k_ref[...] = _rope(k, c, s).astype(k_ref.dtype)
    o += wk
    v_ref[...] = jnp.dot(xb, w_ref[:, o:o + wv], preferred_element_type=F32).astype(v_ref.dtype)
    o += wv
    p_ref[...] = jnp.dot(xb, w_ref[:, o:o + wp], preferred_element_type=F32)
    o += wp
    gate_ref[...] = _sigmoid(jnp.dot(xb, w_ref[:, o:o + wg], preferred_element_type=F32)).astype(gate_ref.dtype)


def _in_proj(x2, g_mix, w_in_b, rope_c, rope_s, widths, tm):
    n, d = x2.shape
    wq, wk, wv, wp, wg = widths
    row = lambda w: pl.BlockSpec((tm, w), lambda i: (i, 0))
    return pl.pallas_call(
        functools.partial(_inproj_kernel, widths=widths),
        grid=(n // tm,),
        in_specs=[row(d), _const_spec((1, d)), _const_spec(w_in_b.shape), row(LANES), row(LANES)],
        out_specs=[row(wq), row(wk), row(wv), row(wp), row(wg)],
        out_shape=[jax.ShapeDtypeStruct((n, wq), BF16), jax.ShapeDtypeStruct((n, wk), BF16),
                   jax.ShapeDtypeStruct((n, wv), BF16), jax.ShapeDtypeStruct((n, wp), F32),
                   jax.ShapeDtypeStruct((n, wg), BF16)],
        compiler_params=_cparams("parallel"),
        name="in_proj",
    )(x2, g_mix.reshape(1, d), w_in_b, rope_c, rope_s)


def _rope_tables(positions):
    half = ROT_DIM // 2
    inv_freq = 1.0 / (ROPE_THETA ** (jnp.arange(0, ROT_DIM, 2, dtype=F32) / ROT_DIM))
    ang = positions.reshape(-1).astype(F32)[:, None] * inv_freq
    cos, sin = jnp.cos(ang), jnp.sin(ang)
    n = ang.shape[0]
    pad1 = jnp.ones((n, HEAD_DIM - ROT_DIM), F32)
    pad0 = jnp.zeros((n, HEAD_DIM - ROT_DIM), F32)
    c = jnp.concatenate([cos, cos, pad1], axis=1)
    s = jnp.concatenate([-sin, sin, pad0], axis=1)
    return jnp.tile(c, (1, LANES // HEAD_DIM)), jnp.tile(s, (1, LANES // HEAD_DIM))


def _swa_kernel(sink_ref, q_ref, kp_ref, kc_ref, vp_ref, vc_ref, o_ref, *, group):
    n = pl.program_id(1)
    blk = q_ref.shape[0]
    scale = HEAD_DIM ** -0.5
    rows = group * blk
    qi = lax.broadcasted_iota(jnp.int32, (rows, 2 * blk), 0) % blk
    kj = lax.broadcasted_iota(jnp.int32, (rows, 2 * blk), 1)
    valid = (kj > qi) & (kj <= qi + WINDOW) & ((kj >= blk) | (n > 0))
    rid = lax.broadcasted_iota(jnp.int32, (rows, 1), 0) // blk
    for kv in range(ATT_KV_HEADS):
        sl = slice(kv * HEAD_DIM, (kv + 1) * HEAD_DIM)
        kk = jnp.concatenate([kp_ref[:, sl], kc_ref[:, sl]], axis=0)
        vv = jnp.concatenate([vp_ref[:, sl], vc_ref[:, sl]], axis=0)
        heads = [kv * group + g for g in range(group)]
        qq = jnp.concatenate([q_ref[:, h * HEAD_DIM:(h + 1) * HEAD_DIM] for h in heads], axis=0)
        s = lax.dot_general(qq, kk, (((1,), (1,)), ((), ())), preferred_element_type=F32) * scale
        s = jnp.where(valid, s, -1e30)
        sink = jnp.zeros((rows, 1), F32)
        for g, h in enumerate(heads):
            sink = jnp.where(rid == g, sink_ref[h], sink)
        m = jnp.maximum(jnp.max(s, axis=-1, keepdims=True), sink)
        p = jnp.exp(s - m)
        denom = jnp.sum(p, axis=-1, keepdims=True) + jnp.exp(sink - m)
        o = jnp.dot((p / denom).astype(vv.dtype), vv, preferred_element_type=F32)
        for g, h in enumerate(heads):
            o_ref[:, h * HEAD_DIM:(h + 1) * HEAD_DIM] = o[g * blk:(g + 1) * blk].astype(o_ref.dtype)


def _swa(q, k, v, sinks, batch, seq):
    n, wq = q.shape
    wk = k.shape[1]
    blk = WINDOW
    nb = seq // blk
    group = (wq // HEAD_DIM) // ATT_KV_HEADS
    cur = lambda w: pl.BlockSpec((blk, w), lambda b, i: (b * nb + i, 0))
    prev = lambda w: pl.BlockSpec((blk, w), lambda b, i: (b * nb + jnp.maximum(i - 1, 0), 0))
    return pl.pallas_call(
        functools.partial(_swa_kernel, group=group),
        grid=(batch, nb),
        in_specs=[pl.BlockSpec(memory_space=pltpu.SMEM), cur(wq), prev(wk), cur(wk), prev(wk), cur(wk)],
        out_specs=cur(wq),
        out_shape=jax.ShapeDtypeStruct((n, wq), BF16),
        compiler_params=_cparams("parallel", "parallel"),
        name="swa",
    )(sinks.astype(F32), q, k, k, v, v)


RW_CHUNK = 64


def _mm(a, b):
    return jnp.dot(a.astype(BF16), b.astype(BF16), preferred_element_type=F32)


def _mm_nt(a, b):
    return lax.dot_general(a.astype(BF16), b.astype(BF16), (((1,), (1,)), ((), ())), preferred_element_type=F32)


def _mm_tn(a, b):
    return lax.dot_general(a.astype(BF16), b.astype(BF16), (((0,), (0,)), ((), ())), preferred_element_type=F32)


def _hi_lo(a):
    hi = a.astype(BF16)
    return hi, (a - hi.astype(F32)).astype(BF16)


def _split_dot(a, b01):
    hi, lo = _hi_lo(a)
    return jnp.dot(hi, b01, preferred_element_type=F32) + jnp.dot(lo, b01, preferred_element_type=F32)


def _split_dot_left(a01, b):
    hi, lo = _hi_lo(b)
    return jnp.dot(a01, hi, preferred_element_type=F32) + jnp.dot(a01, lo, preferred_element_type=F32)


def _rwkv_kernel(p_ref, mu_ref, w0_ref, w2_ref, a0_ref, a2_ref, g2_ref, kk_ref, ka_ref, rk_ref, lnw_ref, lnb_ref,
                 o_ref, last_ref, st_ref, *, width):
    c = pl.program_id(1)
    C = p_ref.shape[0]
    W = width
    N = RWKV_HEAD_DIM
    SL = 2 * N
    n_slab = W // SL

    @pl.when(c == 0)
    def _():
        last_ref[...] = jnp.zeros_like(last_ref)
        st_ref[...] = jnp.zeros_like(st_ref)

    p = p_ref[...]
    row = lax.broadcasted_iota(jnp.int32, (C, 1), 0)
    prev = jnp.where(row == 0, last_ref[...], pltpu.roll(p, 1, axis=0))
    last_ref[...] = p[C - 1:C, :]
    p = p + (prev - p) * mu_ref[...]
    r = p[:, :W]
    k = p[:, W:2 * W]
    v = p[:, 2 * W:3 * W]
    o3 = 3 * W
    dw = p[:, o3:o3 + DECAY_LORA]
    da = p[:, o3 + DECAY_LORA:o3 + DECAY_LORA + AAA_LORA]
    dg = p[:, o3 + DECAY_LORA + AAA_LORA:]

    z = -(w0_ref[...] + _mm(jnp.tanh(dw), w2_ref[...]))
    w_log = -(jnp.maximum(z, 0.0) + jnp.log(1.0 + jnp.exp(-jnp.abs(z)))) - 0.5
    lw = -jnp.exp(w_log)
    a = _sigmoid(a0_ref[...] + _mm(da, a2_ref[...]))
    g = _mm(_sigmoid(dg), g2_ref[...])

    ti = lax.broadcasted_iota(jnp.int32, (C, C), 0)
    tj = lax.broadcasted_iota(jnp.int32, (C, C), 1)
    cum = _split_dot_left((ti >= tj).astype(BF16), lw)
    e_pos = jnp.exp(cum)
    e_neg = jnp.exp(-cum)
    e_prev = jnp.exp(cum - lw)
    e_end = jnp.exp(cum[C - 1:C, :] - cum)
    p_end = e_pos[C - 1:C, :]

    li = lax.broadcasted_iota(jnp.int32, (SL, SL), 0)
    lj = lax.broadcasted_iota(jnp.int32, (SL, SL), 1)
    same = (li // N) == (lj // N)
    head_sum = same.astype(BF16)
    strict = same & ((li % N) > (lj % N))
    incl = same & ((li % N) >= (lj % N))
    lane = lax.broadcasted_iota(jnp.int32, (C, SL), 1)
    m0 = lane < N

    def stack(t):
        return jnp.concatenate([jnp.where(m0, t, 0.0), jnp.where(m0, 0.0, t)], axis=0)

    S = range(n_slab)
    sls = [slice(s * SL, (s + 1) * SL) for s in S]
    h2 = 2 * C
    bf = lambda t: t.astype(BF16)
    x = [k[:, sl] * kk_ref[:, sl] for sl in sls]
    ss = [_split_dot(x[s] * x[s], head_sum) for s in S]
    kk = [x[s] / jnp.maximum(jnp.sqrt(ss[s]), 1e-12) for s in S]
    k2 = [k[:, sl] * (1.0 + (a[:, sl] - 1.0) * ka_ref[:, sl]) for sl in sls]
    kb = [kk[s] * a[:, sls[s]] for s in S]
    AR = [bf(jnp.concatenate([stack(-kk[s] * e_prev[:, sls[s]]), stack(r[:, sls[s]] * e_pos[:, sls[s]])], axis=0))
          for s in S]
    BK = [bf(jnp.concatenate([stack(kb[s] * e_neg[:, sls[s]]), stack(k2[s] * e_neg[:, sls[s]])], axis=0)) for s in S]
    V2 = [bf(stack(v[:, sl])) for sl in sls]
    Be = [bf(stack(kb[s] * e_end[:, sls[s]])) for s in S]
    Ke = [bf(stack(k2[s] * e_end[:, sls[s]])) for s in S]
    st = [st_ref[s] for s in S]
    G = [_mm_nt(AR[s], BK[s]) for s in S]
    ST = [_mm(AR[s], st[s]) for s in S]
    A_ak = [bf(jnp.where(strict, G[s][:h2, h2:], 0.0)) for s in S]
    U = [ST[s][:h2] + _mm(A_ak[s], V2[s]) for s in S]
    Ap = [bf(jnp.where(strict, G[s][:h2, :h2], 0.0)) for s in S]
    steps = int(math.log2(C))
    for i in range(steps):
        U = [U[s] + _mm(Ap[s], U[s]) for s in S]
        if i + 1 < steps:
            Ap = [bf(_mm(Ap[s], Ap[s])) for s in S]
    Ub = [bf(U[s]) for s in S]
    A_rb = [bf(jnp.where(incl, G[s][h2:, :h2], 0.0)) for s in S]
    A_rk = [bf(jnp.where(incl, G[s][h2:, h2:], 0.0)) for s in S]
    O2 = [ST[s][h2:] + _mm(A_rb[s], Ub[s]) + _mm(A_rk[s], V2[s]) for s in S]
    for s in S:
        decay = jnp.transpose(jnp.broadcast_to(p_end[:, sls[s]], (SL, SL)))
        st_new = st[s] * decay + _mm_tn(Be[s], Ub[s]) + _mm_tn(Ke[s], V2[s])
        st_ref[s] = jnp.where(same, st_new, 0.0)
    o = [O2[s][:C] + O2[s][C:] for s in S]
    mean = [_split_dot(o[s], head_sum) * (1.0 / N) for s in S]
    dlt = [o[s] - mean[s] for s in S]
    var = [_split_dot(dlt[s] * dlt[s], head_sum) * (1.0 / N) for s in S]
    bonus = [_split_dot(r[:, sls[s]] * k2[s] * rk_ref[:, sls[s]], head_sum) * v[:, sls[s]] for s in S]
    for s in S:
        sl = sls[s]
        on = dlt[s] * lax.rsqrt(var[s] + RWKV_GN_EPS) * lnw_ref[:, sl] + lnb_ref[:, sl]
        o_ref[:, sl] = ((on + bonus[s]) * g[:, sl]).astype(o_ref.dtype)


def _rwkv(p, mu, w0, w2, a0, a2, g2, k_k, k_a, r_k, ln_w, ln_b, batch, seq):
    n, wp = p.shape
    width = w0.shape[-1]
    C = RW_CHUNK
    nc = seq // C
    vec = lambda t: t.reshape(1, -1).astype(F32)
    args = [vec(mu), vec(w0), w2.astype(BF16), vec(a0), a2.astype(BF16), g2.astype(BF16), vec(k_k), vec(k_a),
            vec(r_k), vec(ln_w), vec(ln_b)]
    return pl.pallas_call(
        functools.partial(_rwkv_kernel, width=width),
        grid=(batch, nc),
        in_specs=[pl.BlockSpec((C, wp), lambda b, c: (b * nc + c, 0))] + [_const_spec(t.shape) for t in args],
        out_specs=pl.BlockSpec((C, width), lambda b, c: (b * nc + c, 0)),
        out_shape=jax.ShapeDtypeStruct((n, width), BF16),
        scratch_shapes=[pltpu.VMEM((1, wp), F32),
                        pltpu.VMEM((width // (2 * RWKV_HEAD_DIM), 2 * RWKV_HEAD_DIM, 2 * RWKV_HEAD_DIM), F32)],
        compiler_params=_cparams("parallel", "arbitrary"),
        name="rwkv",
    )(p, *args)


def _memkv_kernel(m_ref, g_ref, w_ref, o_ref):
    o_ref[...] = jnp.dot(_rms(m_ref[...], g_ref[...]).astype(BF16), w_ref[...],
                         preferred_element_type=F32).astype(o_ref.dtype)


def _mem_kv(mem2, g_mem, w_kv_b, mem_len):
    n, d = mem2.shape
    wo = w_kv_b.shape[1]
    return pl.pallas_call(
        _memkv_kernel,
        grid=(n // mem_len,),
        in_specs=[pl.BlockSpec((mem_len, d), lambda i: (i, 0)), _const_spec((1, d)), _const_spec(w_kv_b.shape)],
        out_specs=pl.BlockSpec((mem_len, wo), lambda i: (i, 0)),
        out_shape=jax.ShapeDtypeStruct((n, wo), BF16),
        compiler_params=_cparams("parallel"),
        name="mem_kv",
    )(mem2, g_mem.reshape(1, d), w_kv_b)


def _post_kernel(x_ref, at_ref, rw_ref, gt_ref, kv_ref, wo_ref, gc_ref, wq_ref, woc_ref, gf_ref, wpq_ref,
                 k1_ref, k2_ref, h_ref, s1_ref, s2_ref):
    d = x_ref.shape[1]
    ga = gt_ref[:, :d].astype(F32)
    gb = gt_ref[:, d:].astype(F32)
    mixed = ga * at_ref[...].astype(F32) + gb * rw_ref[...].astype(F32)
    h1 = x_ref[...] + jnp.dot(mixed.astype(BF16), wo_ref[...], preferred_element_type=F32)

    qc = jnp.dot(_rms(h1, gc_ref[...]).astype(BF16), wq_ref[...], preferred_element_type=F32)
    wc = qc.shape[1]
    hd = wc // CROSS_HEADS
    scale = hd ** -0.5
    outs = []
    for hh in range(CROSS_HEADS):
        qh = (qc[:, hh * hd:(hh + 1) * hd] * scale).astype(BF16)
        kh = kv_ref[:, hh * hd:(hh + 1) * hd]
        vh = kv_ref[:, wc + hh * hd:wc + (hh + 1) * hd]
        s = lax.dot_general(qh, kh, (((1,), (1,)), ((), ())), preferred_element_type=F32)
        pr = jnp.exp(s - jnp.max(s, axis=-1, keepdims=True))
        pr = pr / jnp.sum(pr, axis=-1, keepdims=True)
        outs.append(jnp.dot(pr.astype(BF16), vh, preferred_element_type=F32))
    oc = jnp.concatenate(outs, axis=1)
    h2 = h1 + jnp.dot(oc.astype(BF16), woc_ref[...], preferred_element_type=F32)
    h_ref[...] = h2

    q3 = jnp.dot(_rms(h2, gf_ref[...]).astype(BF16), wpq_ref[...], preferred_element_type=F32)
    half = k1_ref.shape[1]
    for hh in range(PEER_HEADS):
        qa = q3[:, 2 * hh * half:(2 * hh + 1) * half]
        qb = q3[:, (2 * hh + 1) * half:(2 * hh + 2) * half]
        dn = (((1,), (1,)), ((), ()))
        s1_ref[hh] = lax.dot_general(k1_ref[...], qa, dn, precision=lax.Precision.HIGHEST, preferred_element_type=F32)
        s2_ref[hh] = lax.dot_general(k2_ref[...], qb, dn, precision=lax.Precision.HIGHEST, preferred_element_type=F32)


def _post(x2, attn, rw, gates, kv, w_out_b, g_cross, w_qc_b, w_oc_b, g_ffn, w_pq_b, k1, k2, batch, seq, mem_len, tm):
    n, d = x2.shape
    nt = seq // tm
    n_keys = k1.shape[0]
    row = lambda w: pl.BlockSpec((tm, w), lambda b, i: (b * nt + i, 0))
    sc_spec = pl.BlockSpec((PEER_HEADS, n_keys, tm), lambda b, i: (0, 0, b * nt + i))
    consts = [w_out_b, g_cross.reshape(1, d), w_qc_b, w_oc_b, g_ffn.reshape(1, d), w_pq_b, k1.astype(F32), k2.astype(F32)]
    return pl.pallas_call(
        _post_kernel,
        grid=(batch, nt),
        in_specs=[row(d), row(d), row(d), row(2 * d), pl.BlockSpec((mem_len, kv.shape[1]), lambda b, i: (b, 0))]
                 + [_const_spec(t.shape) for t in consts],
        out_specs=[row(d), sc_spec, sc_spec],
        out_shape=[jax.ShapeDtypeStruct((n, d), F32), jax.ShapeDtypeStruct((PEER_HEADS, n_keys, n), F32),
                   jax.ShapeDtypeStruct((PEER_HEADS, n_keys, n), F32)],
        compiler_params=_cparams("parallel", "parallel"),
        name="post",
    )(x2, attn, rw, gates, kv, *consts)


ROUTE_UNROLL = 4


def _topk_rows(s, k, payload=None):
    rows = s.shape[0]
    rid = lax.broadcasted_iota(jnp.int32, s.shape, 0).astype(F32)
    vals, outs = [], []
    for _ in range(k):
        m = jnp.max(s, axis=0, keepdims=True)
        i = jnp.min(jnp.where(s == m, rid, float(rows)), axis=0, keepdims=True)
        hit = rid == i
        vals.append(m)
        outs.append(i if payload is None else jnp.max(jnp.where(hit, payload, -1.0), axis=0, keepdims=True))
        s = jnp.where(hit, -jnp.inf, s)
    return jnp.concatenate(vals, axis=0), jnp.concatenate(outs, axis=0)


def _route_kernel(s1_ref, s2_ref, e_ref, g_ref):
    K = PEER_TOPK
    n_keys = s1_ref.shape[1]

    def one_head(h):
        v1, i1 = _topk_rows(s1_ref[h], K)
        v2, i2 = _topk_rows(s2_ref[h], K)
        cand, ecand = [], []
        for a in range(K):
            nb = K // (a + 1)
            cand.append(v1[a:a + 1, :] + v2[:nb, :])
            ecand.append(i1[a:a + 1, :] * float(n_keys) + i2[:nb, :])
        n_c = sum(c.shape[0] for c in cand)
        pad = -n_c % SUBLANES
        if pad:
            cand.append(jnp.full((pad, v1.shape[1]), -jnp.inf, F32))
            ecand.append(jnp.full((pad, v1.shape[1]), -1.0, F32))
        sc, e = _topk_rows(jnp.concatenate(cand, axis=0), K, jnp.concatenate(ecand, axis=0))
        e_ref[h] = e.astype(jnp.int32)
        ex = jnp.exp(sc - sc[0:1, :])
        g_ref[h] = ex / jnp.sum(ex, axis=0, keepdims=True)

    def body(i, carry):
        for j in range(ROUTE_UNROLL):
            one_head(i * ROUTE_UNROLL + j)
        return carry

    lax.fori_loop(0, s1_ref.shape[0] // ROUTE_UNROLL, body, 0)


def _route(s1, s2, tk):
    heads, n_keys, n = s1.shape
    spec_in = pl.BlockSpec((heads, n_keys, tk), lambda i: (0, 0, i))
    spec_out = pl.BlockSpec((heads, PEER_TOPK, tk), lambda i: (0, 0, i))
    return pl.pallas_call(
        _route_kernel,
        grid=(n // tk,),
        in_specs=[spec_in, spec_in],
        out_specs=[spec_out, spec_out],
        out_shape=[jax.ShapeDtypeStruct((heads, PEER_TOPK, n), jnp.int32),
                   jax.ShapeDtypeStruct((heads, PEER_TOPK, n), F32)],
        compiler_params=_cparams("parallel"),
        name="route",
    )(s1, s2)


PEER_GROUP = 16
SUB = SUBLANES
PEER_UNROLL = 4


def _gather_rows(tbl_ref, e_ref, t, g):
    parts = [tbl_ref[pl.ds(pl.multiple_of(e_ref[t, g * PEER_GROUP + j], 4), 4), :] for j in range(PEER_GROUP)]
    return pltpu.bitcast(jnp.concatenate(parts, axis=0), BF16)


def _diag_mask(shape):
    r = lax.broadcasted_iota(jnp.int32, shape, len(shape) - 2)
    c = lax.broadcasted_iota(jnp.int32, shape, len(shape) - 1)
    return r == (c % SUB)


def _peer_act_kernel(e_ref, h_ref, gf_ref, gate_ref, tbl_ref, sel_ref, o_ref, r_ref, xn_ref):
    tm = h_ref.shape[0]
    n_grp = e_ref.shape[1] // PEER_GROUP
    h = h_ref[...]
    ms = jnp.sum(jnp.sum(h * h, axis=2, keepdims=True), axis=1, keepdims=True) / (h.shape[1] * h.shape[2])
    xn_ref[...] = h * lax.rsqrt(ms + RMS_EPS) * gf_ref[...]

    def body(i, carry):
        for j in range(PEER_UNROLL):
            t = i * PEER_UNROLL + j
            xt = xn_ref[t].astype(BF16)
            for g in range(n_grp):
                w = _gather_rows(tbl_ref, e_ref, t, g)
                r_ref[t, :, g * LANES:(g + 1) * LANES] = lax.dot_general(
                    xt, w, (((1,), (1,)), ((), ())), preferred_element_type=F32)
        return carry

    lax.fori_loop(0, tm // PEER_UNROLL, body, 0)
    r = r_ref[...]
    dots = jnp.sum(jnp.where(_diag_mask(r.shape), r, 0.0), axis=1)
    act = _split_dot(dots, sel_ref[...])
    gelu = 0.5 * act * (1.0 + lax.erf(act * (2.0 ** -0.5)))
    o_ref[...] = gate_ref[...] * gelu


def _peer_out_kernel(e_ref, h_ref, gw_ref, tbl_ref, exp_ref, gfin_ref, o_ref, ge_ref, acc_ref):
    tm = h_ref.shape[0]
    n_grp = e_ref.shape[1] // PEER_GROUP
    ge_ref[...] = _split_dot(gw_ref[...], exp_ref[...])
    mask = _diag_mask((SUB, n_grp * LANES))

    def body(i, carry):
        for j in range(PEER_UNROLL):
            t = i * PEER_UNROLL + j
            lhs = jnp.where(mask, jnp.broadcast_to(ge_ref[pl.ds(t, 1), :], mask.shape), 0.0).astype(BF16)
            acc = jnp.zeros((SUB, LANES), F32)
            for g in range(n_grp):
                w = _gather_rows(tbl_ref, e_ref, t, g)
                acc = acc + jnp.dot(lhs[:, g * LANES:(g + 1) * LANES], w, preferred_element_type=F32)
            acc_ref[t] = acc
        return carry

    lax.fori_loop(0, tm // PEER_UNROLL, body, 0)
    y = h_ref[...] + acc_ref[...]
    ms = jnp.sum(jnp.sum(y * y, axis=2, keepdims=True), axis=1, keepdims=True) / (y.shape[1] * y.shape[2])
    o_ref[...] = y * lax.rsqrt(ms + RMS_EPS) * gfin_ref[...]


def _table_tiles(t):
    e, d = t.shape
    p = d // (2 * LANES)
    tb = t.astype(BF16).reshape(e, p, 2, LANES).transpose(0, 1, 3, 2)
    return lax.bitcast_convert_type(tb, jnp.int32).reshape(e * p, LANES)


def _peer(h2, e_idx, gate, g_ffn, g_final, u_tiles, v_tiles, tm):
    n, d = h2.shape
    kk = e_idx.shape[1]
    h3 = h2.reshape(n, SUB, d // SUB)
    row3 = pl.BlockSpec((tm, SUB, d // SUB), lambda i: (i, 0, 0))
    row = pl.BlockSpec((tm, kk), lambda i: (i, 0))
    idx = pl.BlockSpec((tm, kk), lambda i: (i, 0), memory_space=pltpu.SMEM)
    lane = jnp.arange(kk * SUB) // SUB
    sel = (lane[:, None] == jnp.arange(kk)[None, :]).astype(BF16)
    gw = pl.pallas_call(
        _peer_act_kernel,
        grid=(n // tm,),
        in_specs=[idx, row3, _const_spec((1, SUB, d // SUB)), row, _const_spec(u_tiles.shape), _const_spec(sel.shape)],
        out_specs=row,
        out_shape=jax.ShapeDtypeStruct((n, kk), F32),
        scratch_shapes=[pltpu.VMEM((tm, SUB, kk * SUB), F32), pltpu.VMEM((tm, SUB, d // SUB), F32)],
        compiler_params=_cparams("arbitrary"),
        name="peer_act",
    )(e_idx, h3, g_ffn.reshape(1, SUB, d // SUB), gate, u_tiles, sel)
    out = pl.pallas_call(
        _peer_out_kernel,
        grid=(n // tm,),
        in_specs=[idx, row3, row, _const_spec(v_tiles.shape), _const_spec(sel.T.shape), _const_spec((1, SUB, d // SUB))],
        out_specs=row3,
        out_shape=jax.ShapeDtypeStruct((n, SUB, d // SUB), F32),
        scratch_shapes=[pltpu.VMEM((tm, kk * SUB), F32), pltpu.VMEM((tm, SUB, d // SUB), F32)],
        compiler_params=_cparams("arbitrary"),
        name="peer_out",
    )(e_idx, h3, gw, v_tiles, sel.T, g_final.reshape(1, SUB, d // SUB))
    return out.reshape(n, d)


def kernel(x, mem, positions, g_mix, w_in, attn_sinks, rwkv_mu, rwkv_w0, rwkv_w2, rwkv_a0, rwkv_a2, rwkv_g2, rwkv_k_k, rwkv_k_a, rwkv_r_k, rwkv_ln_w, rwkv_ln_b, w_out, g_cross, g_mem, w_q_cross, w_kv_cross, w_o_cross, g_ffn, peer_w_q, peer_sub_keys_1, peer_sub_keys_2, peer_u, peer_v, g_final):
    b, s, d = x.shape
    n = b * s
    L = 0
    wq = d
    wk = ATT_KV_HEADS * HEAD_DIM
    wp = 3 * d + DECAY_LORA + AAA_LORA + GATE_LORA
    widths = (wq, wk, wk, wp, 2 * d)
    rc, rs = _rope_tables(positions)
    q, k, v, p, gates = _in_proj(x.reshape(n, d), g_mix[L], w_in[L].astype(BF16), rc, rs, widths, 256)
    attn = _swa(q, k, v, attn_sinks[L], b, s)
    rw = _rwkv(p, rwkv_mu[L], rwkv_w0[L], rwkv_w2[L], rwkv_a0[L], rwkv_a2[L], rwkv_g2[L], rwkv_k_k[L], rwkv_k_a[L],
               rwkv_r_k[L], rwkv_ln_w[L], rwkv_ln_b[L], b, s)
    m = mem.shape[1]
    kv = _mem_kv(mem.reshape(b * m, d), g_mem[L], w_kv_cross[L].astype(BF16), m)
    h2, s1, s2 = _post(x.reshape(n, d), attn, rw, gates, kv, w_out[L].astype(BF16), g_cross[L],
                       w_q_cross[L].astype(BF16), w_o_cross[L].astype(BF16), g_ffn[L], peer_w_q[L].astype(BF16),
                       peer_sub_keys_1[L], peer_sub_keys_2[L], b, s, m, 256)
    e_t, g_t = _route(s1, s2, LANES)
    kk = PEER_HEADS * PEER_TOPK
    e_idx = e_t.reshape(kk, n).T * 4
    gate = g_t.reshape(kk, n).T
    out = _peer(h2, e_idx, gate, g_ffn[L], g_final, _table_tiles(peer_u[L]), _table_tiles(peer_v[L]), 64)
    return out.reshape(b, s, d)
```

```python
import functools
import math

import jax
import jax.numpy as jnp
from jax import lax
from jax.experimental import pallas as pl
from jax.experimental.pallas import tpu as pltpu

F32 = jnp.float32
BF16 = jnp.bfloat16

RMS_EPS = 1e-5
HEAD_DIM = 64
ATT_KV_HEADS = 4
WINDOW = 128
ROT_DIM = HEAD_DIM // 4
ROPE_THETA = 500000.0
RWKV_HEAD_DIM = 64
DECAY_LORA = 64
AAA_LORA = 64
GATE_LORA = 128
RWKV_GN_EPS = 64e-5
CROSS_HEADS = 4
PEER_HEADS = 8
PEER_TOPK = 16

LANES = 128
SUBLANES = 8
VMEM_LIMIT = 56 * 1024 * 1024


def _cparams(*sem):
    return pltpu.CompilerParams(dimension_semantics=sem, vmem_limit_bytes=VMEM_LIMIT)


def _const_spec(shape):
    nd = len(shape)
    return pl.BlockSpec(shape, lambda *_: (0,) * nd, pipeline_mode=pl.Buffered(1))


def _rms(x, g):
    return x * lax.rsqrt(jnp.mean(x * x, axis=-1, keepdims=True) + RMS_EPS) * g


def _sigmoid(x):
    return 1.0 / (1.0 + jnp.exp(-x))


def _rope(t, c, s):
    w = t.shape[-1]
    rep = w // LANES
    cc = jnp.tile(c, (1, rep))
    ss = jnp.tile(s, (1, rep))
    lane = lax.broadcasted_iota(jnp.int32, t.shape, 1)
    first = (lane % HEAD_DIM) < (ROT_DIM // 2)
    partner = jnp.where(first, pltpu.roll(t, w - ROT_DIM // 2, axis=1), pltpu.roll(t, ROT_DIM // 2, axis=1))
    return t * cc + partner * ss


def _inproj_kernel(x_ref, g_ref, w_ref, c_ref, s_ref, q_ref, k_ref, v_ref, p_ref, gate_ref, *, widths):
    wq, wk, wv, wp, wg = widths
    xb = _rms(x_ref[...], g_ref[...]).astype(BF16)
    c = c_ref[...]
    s = s_ref[...]
    o = 0
    q = jnp.dot(xb, w_ref[:, o:o + wq], preferred_element_type=F32)
    q_ref[...] = _rope(q, c, s).astype(q_ref.dtype)
    o += wq
    k = jnp.dot(xb, w_ref[:, o:o + wk], preferred_element_type=F32)
    k_ref[...] = _rope(k, c, s).astype(k_ref.dtype)
    o += wk
    v_ref[...] = jnp.dot(xb, w_ref[:, o:o + wv], preferred_element_type=F32).astype(v_ref.dtype)
    o += wv
    p_ref[...] = jnp.dot(xb, w_ref[:, o:o + wp], preferred_element_type=F32)
    o += wp
    gate_ref[...] = _sigmoid(jnp.dot(xb, w_ref[:, o:o + wg], preferred_element_type=F32)).astype(gate_ref.dtype)


def _in_proj(x2, g_mix, w_in_b, rope_c, rope_s, widths, tm):
    n, d = x2.shape
    wq, wk, wv, wp, wg = widths
    row = lambda w: pl.BlockSpec((tm, w), lambda i: (i, 0))
    return pl.pallas_call(
        functools.partial(_inproj_kernel, widths=widths),
        grid=(n // tm,),
        in_specs=[row(d), _const_spec((1, d)), _const_spec(w_in_b.shape), row(LANES), row(LANES)],
        out_specs=[row(wq), row(wk), row(wv), row(wp), row(wg)],
        out_shape=[jax.ShapeDtypeStruct((n, wq), BF16), jax.ShapeDtypeStruct((n, wk), BF16),
                   jax.ShapeDtypeStruct((n, wv), BF16), jax.ShapeDtypeStruct((n, wp), F32),
                   jax.ShapeDtypeStruct((n, wg), BF16)],
        compiler_params=_cparams("parallel"),
        name="in_proj",
    )(x2, g_mix.reshape(1, d), w_in_b, rope_c, rope_s)


def _rope_tables(positions):
    half = ROT_DIM // 2
    inv_freq = 1.0 / (ROPE_THETA ** (jnp.arange(0, ROT_DIM, 2, dtype=F32) / ROT_DIM))
    ang = positions.reshape(-1).astype(F32)[:, None] * inv_freq
    cos, sin = jnp.cos(ang), jnp.sin(ang)
    n = ang.shape[0]
    pad1 = jnp.ones((n, HEAD_DIM - ROT_DIM), F32)
    pad0 = jnp.zeros((n, HEAD_DIM - ROT_DIM), F32)
    c = jnp.concatenate([cos, cos, pad1], axis=1)
    s = jnp.concatenate([-sin, sin, pad0], axis=1)
    return jnp.tile(c, (1, LANES // HEAD_DIM)), jnp.tile(s, (1, LANES // HEAD_DIM))


def _swa_kernel(sink_ref, q_ref, kp_ref, kc_ref, vp_ref, vc_ref, o_ref, *, group):
    n = pl.program_id(1)
    blk = q_ref.shape[0]
    scale = HEAD_DIM ** -0.5
    rows = group * blk
    qi = lax.broadcasted_iota(jnp.int32, (rows, 2 * blk), 0) % blk
    kj = lax.broadcasted_iota(jnp.int32, (rows, 2 * blk), 1)
    valid = (kj > qi) & (kj <= qi + WINDOW) & ((kj >= blk) | (n > 0))
    rid = lax.broadcasted_iota(jnp.int32, (rows, 1), 0) // blk
    for kv in range(ATT_KV_HEADS):
        sl = slice(kv * HEAD_DIM, (kv + 1) * HEAD_DIM)
        kk = jnp.concatenate([kp_ref[:, sl], kc_ref[:, sl]], axis=0)
        vv = jnp.concatenate([vp_ref[:, sl], vc_ref[:, sl]], axis=0)
        heads = [kv * group + g for g in range(group)]
        qq = jnp.concatenate([q_ref[:, h * HEAD_DIM:(h + 1) * HEAD_DIM] for h in heads], axis=0)
        s = lax.dot_general(qq, kk, (((1,), (1,)), ((), ())), preferred_element_type=F32) * scale
        s = jnp.where(valid, s, -1e30)
        sink = jnp.zeros((rows, 1), F32)
        for g, h in enumerate(heads):
            sink = jnp.where(rid == g, sink_ref[h], sink)
        m = jnp.maximum(jnp.max(s, axis=-1, keepdims=True), sink)
        p = jnp.exp(s - m)
        denom = jnp.sum(p, axis=-1, keepdims=True) + jnp.exp(sink - m)
        o = jnp.dot((p / denom).astype(vv.dtype), vv, preferred_element_type=F32)
        for g, h in enumerate(heads):
            o_ref[:, h * HEAD_DIM:(h + 1) * HEAD_DIM] = o[g * blk:(g + 1) * blk].astype(o_ref.dtype)


def _swa(q, k, v, sinks, batch, seq):
    n, wq = q.shape
    wk = k.shape[1]
    blk = WINDOW
    nb = seq // blk
    group = (wq // HEAD_DIM) // ATT_KV_HEADS
    cur = lambda w: pl.BlockSpec((blk, w), lambda b, i: (b * nb + i, 0))
    prev = lambda w: pl.BlockSpec((blk, w), lambda b, i: (b * nb + jnp.maximum(i - 1, 0), 0))
    return pl.pallas_call(
        functools.partial(_swa_kernel, group=group),
        grid=(batch, nb),
        in_specs=[pl.BlockSpec(memory_space=pltpu.SMEM), cur(wq), prev(wk), cur(wk), prev(wk), cur(wk)],
        out_specs=cur(wq),
        out_shape=jax.ShapeDtypeStruct((n, wq), BF16),
        compiler_params=_cparams("parallel", "parallel"),
        name="swa",
    )(sinks.astype(F32), q, k, k, v, v)


RW_CHUNK = 64


def _mm(a, b):
    return jnp.dot(a.astype(BF16), b.astype(BF16), preferred_element_type=F32)


def _mm_nt(a, b):
    return lax.dot_general(a.astype(BF16), b.astype(BF16), (((1,), (1,)), ((), ())), preferred_element_type=F32)


def _mm_tn(a, b):
    return lax.dot_general(a.astype(BF16), b.astype(BF16), (((0,), (0,)), ((), ())), preferred_element_type=F32)


def _hi_lo(a):
    hi = a.astype(BF16)
    return hi, (a - hi.astype(F32)).astype(BF16)


def _split_dot(a, b01):
    hi, lo = _hi_lo(a)
    return jnp.dot(hi, b01, preferred_element_type=F32) + jnp.dot(lo, b01, preferred_element_type=F32)


def _split_dot_left(a01, b):
    hi, lo = _hi_lo(b)
    return jnp.dot(a01, hi, preferred_element_type=F32) + jnp.dot(a01, lo, preferred_element_type=F32)


def _rwkv_kernel(p_ref, mu_ref, w0_ref, w2_ref, a0_ref, a2_ref, g2_ref, kk_ref, ka_ref, rk_ref, lnw_ref, lnb_ref,
                 o_ref, last_ref, st_ref, *, width):
    c = pl.program_id(1)
    C = p_ref.shape[0]
    W = width
    N = RWKV_HEAD_DIM
    SL = 2 * N
    n_slab = W // SL

    @pl.when(c == 0)
    def _():
        last_ref[...] = jnp.zeros_like(last_ref)
        st_ref[...] = jnp.zeros_like(st_ref)

    p = p_ref[...]
    row = lax.broadcasted_iota(jnp.int32, (C, 1), 0)
    prev = jnp.where(row == 0, last_ref[...], pltpu.roll(p, 1, axis=0))
    last_ref[...] = p[C - 1:C, :]
    p = p + (prev - p) * mu_ref[...]
    r = p[:, :W]
    k = p[:, W:2 * W]
    v = p[:, 2 * W:3 * W]
    o3 = 3 * W
    dw = p[:, o3:o3 + DECAY_LORA]
    da = p[:, o3 + DECAY_LORA:o3 + DECAY_LORA + AAA_LORA]
    dg = p[:, o3 + DECAY_LORA + AAA_LORA:]

    z = -(w0_ref[...] + _mm(jnp.tanh(dw), w2_ref[...]))
    w_log = -(jnp.maximum(z, 0.0) + jnp.log(1.0 + jnp.exp(-jnp.abs(z)))) - 0.5
    lw = -jnp.exp(w_log)
    a = _sigmoid(a0_ref[...] + _mm(da, a2_ref[...]))
    g = _mm(_sigmoid(dg), g2_ref[...])

    ti = lax.broadcasted_iota(jnp.int32, (C, C), 0)
    tj = lax.broadcasted_iota(jnp.int32, (C, C), 1)
    cum = _split_dot_left((ti >= tj).astype(BF16), lw)
    e_pos = jnp.exp(cum)
    e_neg = jnp.exp(-cum)
    e_prev = jnp.exp(cum - lw)
    e_end = jnp.exp(cum[C - 1:C, :] - cum)
    p_end = e_pos[C - 1:C, :]

    li = lax.broadcasted_iota(jnp.int32, (SL, SL), 0)
    lj = lax.broadcasted_iota(jnp.int32, (SL, SL), 1)
    same = (li // N) == (lj // N)
    head_sum = same.astype(BF16)
    strict = same & ((li % N) > (lj % N))
    incl = same & ((li % N) >= (lj % N))
    lane = lax.broadcasted_iota(jnp.int32, (C, SL), 1)
    m0 = lane < N

    def stack(t):
        return jnp.concatenate([jnp.where(m0, t, 0.0), jnp.where(m0, 0.0, t)], axis=0)

    S = range(n_slab)
    sls = [slice(s * SL, (s + 1) * SL) for s in S]
    h2 = 2 * C
    bf = lambda t: t.astype(BF16)
    x = [k[:, sl] * kk_ref[:, sl] for sl in sls]
    ss = [_split_dot(x[s] * x[s], head_sum) for s in S]
    kk = [x[s] / jnp.maximum(jnp.sqrt(ss[s]), 1e-12) for s in S]
    k2 = [k[:, sl] * (1.0 + (a[:, sl] - 1.0) * ka_ref[:, sl]) for sl in sls]
    kb = [kk[s] * a[:, sls[s]] for s in S]
    AR = [bf(jnp.concatenate([stack(-kk[s] * e_prev[:, sls[s]]), stack(r[:, sls[s]] * e_pos[:, sls[s]])], axis=0))
          for s in S]
    BK = [bf(jnp.concatenate([stack(kb[s] * e_neg[:, sls[s]]), stack(k2[s] * e_neg[:, sls[s]])], axis=0)) for s in S]
    V2 = [bf(stack(v[:, sl])) for sl in sls]
    Be = [bf(stack(kb[s] * e_end[:, sls[s]])) for s in S]
    Ke = [bf(stack(k2[s] * e_end[:, sls[s]])) for s in S]
    st = [st_ref[s] for s in S]
    G = [_mm_nt(AR[s], BK[s]) for s in S]
    ST = [_mm(AR[s], st[s]) for s in S]
    A_ak = [bf(jnp.where(strict, G[s][:h2, h2:], 0.0)) for s in S]
    U = [ST[s][:h2] + _mm(A_ak[s], V2[s]) for s in S]
    Ap = [bf(jnp.where(strict, G[s][:h2, :h2], 0.0)) for s in S]
    steps = int(math.log2(C))
    for i in range(steps):
        U = [U[s] + _mm(Ap[s], U[s]) for s in S]
        if i + 1 < steps:
            Ap = [bf(_mm(Ap[s], Ap[s])) for s in S]
    Ub = [bf(U[s]) for s in S]
    A_rb = [bf(jnp.where(incl, G[s][h2:, :h2], 0.0)) for s in S]
    A_rk = [bf(jnp.where(incl, G[s][h2:, h2:], 0.0)) for s in S]
    O2 = [ST[s][h2:] + _mm(A_rb[s], Ub[s]) + _mm(A_rk[s], V2[s]) for s in S]
    for s in S:
        decay = jnp.transpose(jnp.broadcast_to(p_end[:, sls[s]], (SL, SL)))
        st_new = st[s] * decay + _mm_tn(Be[s], Ub[s]) + _mm_tn(Ke[s], V2[s])
        st_ref[s] = jnp.where(same, st_new, 0.0)
    o = [O2[s][:C] + O2[s][C:] for s in S]
    mean = [_split_dot(o[s], head_sum) * (1.0 / N) for s in S]
    dlt = [o[s] - mean[s] for s in S]
    var = [_split_dot(dlt[s] * dlt[s], head_sum) * (1.0 / N) for s in S]
    bonus = [_split_dot(r[:, sls[s]] * k2[s] * rk_ref[:, sls[s]], head_sum) * v[:, sls[s]] for s in S]
    for s in S:
        sl = sls[s]
        on = dlt[s] * lax.rsqrt(var[s] + RWKV_GN_EPS) * lnw_ref[:, sl] + lnb_ref[:, sl]
        o_ref[:, sl] = ((on + bonus[s]) * g[:, sl]).astype(o_ref.dtype)


def _rwkv(p, mu, w0, w2, a0, a2, g2, k_k, k_a, r_k, ln_w, ln_b, batch, seq):
    n, wp = p.shape
    width = w0.shape[-1]
    C = RW_CHUNK
    nc = seq // C
    vec = lambda t: t.reshape(1, -1).astype(F32)
    args = [vec(mu), vec(w0), w2.astype(BF16), vec(a0), a2.astype(BF16), g2.astype(BF16), vec(k_k), vec(k_a),
            vec(r_k), vec(ln_w), vec(ln_b)]
    return pl.pallas_call(
        functools.partial(_rwkv_kernel, width=width),
        grid=(batch, nc),
        in_specs=[pl.BlockSpec((C, wp), lambda b, c: (b * nc + c, 0))] + [_const_spec(t.shape) for t in args],
        out_specs=pl.BlockSpec((C, width), lambda b, c: (b * nc + c, 0)),
        out_shape=jax.ShapeDtypeStruct((n, width), BF16),
        scratch_shapes=[pltpu.VMEM((1, wp), F32),
                        pltpu.VMEM((width // (2 * RWKV_HEAD_DIM), 2 * RWKV_HEAD_DIM, 2 * RWKV_HEAD_DIM), F32)],
        compiler_params=_cparams("parallel", "arbitrary"),
        name="rwkv",
    )(p, *args)


def _memkv_kernel(m_ref, g_ref, w_ref, o_ref):
    o_ref[...] = jnp.dot(_rms(m_ref[...], g_ref[...]).astype(BF16), w_ref[...],
                         preferred_element_type=F32).astype(o_ref.dtype)


def _mem_kv(mem2, g_mem, w_kv_b, mem_len):
    n, d = mem2.shape
    wo = w_kv_b.shape[1]
    return pl.pallas_call(
        _memkv_kernel,
        grid=(n // mem_len,),
        in_specs=[pl.BlockSpec((mem_len, d), lambda i: (i, 0)), _const_spec((1, d)), _const_spec(w_kv_b.shape)],
        out_specs=pl.BlockSpec((mem_len, wo), lambda i: (i, 0)),
        out_shape=jax.ShapeDtypeStruct((n, wo), BF16),
        compiler_params=_cparams("parallel"),
        name="mem_kv",
    )(mem2, g_mem.reshape(1, d), w_kv_b)


def _post_kernel(x_ref, at_ref, rw_ref, gt_ref, kv_ref, wo_ref, gc_ref, wq_ref, woc_ref, gf_ref, wpq_ref,
                 k1_ref, k2_ref, h_ref, s1_ref, s2_ref):
    d = x_ref.shape[1]
    ga = gt_ref[:, :d].astype(F32)
    gb = gt_ref[:, d:].astype(F32)
    mixed = ga * at_ref[...].astype(F32) + gb * rw_ref[...].astype(F32)
    h1 = x_ref[...] + jnp.dot(mixed.astype(BF16), wo_ref[...], preferred_element_type=F32)

    qc = jnp.dot(_rms(h1, gc_ref[...]).astype(BF16), wq_ref[...], preferred_element_type=F32)
    wc = qc.shape[1]
    hd = wc // CROSS_HEADS
    scale = hd ** -0.5
    outs = []
    for hh in range(CROSS_HEADS):
        qh = (qc[:, hh * hd:(hh + 1) * hd] * scale).astype(BF16)
        kh = kv_ref[:, hh * hd:(hh + 1) * hd]
        vh = kv_ref[:, wc + hh * hd:wc + (hh + 1) * hd]
        s = lax.dot_general(qh, kh, (((1,), (1,)), ((), ())), preferred_element_type=F32)
        pr = jnp.exp(s - jnp.max(s, axis=-1, keepdims=True))
        pr = pr / jnp.sum(pr, axis=-1, keepdims=True)
        outs.append(jnp.dot(pr.astype(BF16), vh, preferred_element_type=F32))
    oc = jnp.concatenate(outs, axis=1)
    h2 = h1 + jnp.dot(oc.astype(BF16), woc_ref[...], preferred_element_type=F32)
    h_ref[...] = h2

    q3 = jnp.dot(_rms(h2, gf_ref[...]).astype(BF16), wpq_ref[...], preferred_element_type=F32)
    half = k1_ref.shape[1]
    for hh in range(PEER_HEADS):
        qa = q3[:, 2 * hh * half:(2 * hh + 1) * half]
        qb = q3[:, (2 * hh + 1) * half:(2 * hh + 2) * half]
        dn = (((1,), (1,)), ((), ()))
        s1_ref[hh] = lax.dot_general(k1_ref[...], qa, dn, precision=lax.Precision.HIGHEST, preferred_element_type=F32)
        s2_ref[hh] = lax.dot_general(k2_ref[...], qb, dn, precision=lax.Precision.HIGHEST, preferred_element_type=F32)


def _post(x2, attn, rw, gates, kv, w_out_b, g_cross, w_qc_b, w_oc_b, g_ffn, w_pq_b, k1, k2, batch, seq, mem_len, tm):
    n, d = x2.shape
    nt = seq // tm
    n_keys = k1.shape[0]
    row = lambda w: pl.BlockSpec((tm, w), lambda b, i: (b * nt + i, 0))
    sc_spec = pl.BlockSpec((PEER_HEADS, n_keys, tm), lambda b, i: (0, 0, b * nt + i))
    consts = [w_out_b, g_cross.reshape(1, d), w_qc_b, w_oc_b, g_ffn.reshape(1, d), w_pq_b, k1.astype(F32), k2.astype(F32)]
    return pl.pallas_call(
        _post_kernel,
        grid=(batch, nt),
        in_specs=[row(d), row(d), row(d), row(2 * d), pl.BlockSpec((mem_len, kv.shape[1]), lambda b, i: (b, 0))]
                 + [_const_spec(t.shape) for t in consts],
        out_specs=[row(d), sc_spec, sc_spec],
        out_shape=[jax.ShapeDtypeStruct((n, d), F32), jax.ShapeDtypeStruct((PEER_HEADS, n_keys, n), F32),
                   jax.ShapeDtypeStruct((PEER_HEADS, n_keys, n), F32)],
        compiler_params=_cparams("parallel", "parallel"),
        name="post",
    )(x2, attn, rw, gates, kv, *consts)


ROUTE_UNROLL = 4


def _topk_rows(s, k, payload=None):
    rows = s.shape[0]
    rid = lax.broadcasted_iota(jnp.int32, s.shape, 0).astype(F32)
    vals, outs = [], []
    for _ in range(k):
        m = jnp.max(s, axis=0, keepdims=True)
        i = jnp.min(jnp.where(s == m, rid, float(rows)), axis=0, keepdims=True)
        hit = rid == i
        vals.append(m)
        outs.append(i if payload is None else jnp.max(jnp.where(hit, payload, -1.0), axis=0, keepdims=True))
        s = jnp.where(hit, -jnp.inf, s)
    return jnp.concatenate(vals, axis=0), jnp.concatenate(outs, axis=0)


def _route_kernel(s1_ref, s2_ref, e_ref, g_ref):
    K = PEER_TOPK
    n_keys = s1_ref.shape[1]

    def one_head(h):
        v1, i1 = _topk_rows(s1_ref[h], K)
        v2, i2 = _topk_rows(s2_ref[h], K)
        cand, ecand = [], []
        for a in range(K):
            nb = K // (a + 1)
            cand.append(v1[a:a + 1, :] + v2[:nb, :])
            ecand.append(i1[a:a + 1, :] * float(n_keys) + i2[:nb, :])
        n_c = sum(c.shape[0] for c in cand)
        pad = -n_c % SUBLANES
        if pad:
            cand.append(jnp.full((pad, v1.shape[1]), -jnp.inf, F32))
            ecand.append(jnp.full((pad, v1.shape[1]), -1.0, F32))
        sc, e = _topk_rows(jnp.concatenate(cand, axis=0), K, jnp.concatenate(ecand, axis=0))
        e_ref[h] = e.astype(jnp.int32)
        ex = jnp.exp(sc - sc[0:1, :])
        g_ref[h] = ex / jnp.sum(ex, axis=0, keepdims=True)

    def body(i, carry):
        for j in range(ROUTE_UNROLL):
            one_head(i * ROUTE_UNROLL + j)
        return carry

    lax.fori_loop(0, s1_ref.shape[0] // ROUTE_UNROLL, body, 0)


def _route(s1, s2, tk):
    heads, n_keys, n = s1.shape
    spec_in = pl.BlockSpec((heads, n_keys, tk), lambda i: (0, 0, i))
    spec_out = pl.BlockSpec((heads, PEER_TOPK, tk), lambda i: (0, 0, i))
    return pl.pallas_call(
        _route_kernel,
        grid=(n // tk,),
        in_specs=[spec_in, spec_in],
        out_specs=[spec_out, spec_out],
        out_shape=[jax.ShapeDtypeStruct((heads, PEER_TOPK, n), jnp.int32),
                   jax.ShapeDtypeStruct((heads, PEER_TOPK, n), F32)],
        compiler_params=_cparams("parallel"),
        name="route",
    )(s1, s2)


PEER_GROUP = 16
SUB = SUBLANES
PEER_UNROLL = 16


def _gather_rows(tbl_ref, e_ref, t, g):
    parts = [tbl_ref[pl.ds(pl.multiple_of(e_ref.at[g * PEER_GROUP + j][t], 4), 4), :] for j in range(PEER_GROUP)]
    return pltpu.bitcast(jnp.concatenate(parts, axis=0), BF16)


def _diag_mask(shape):
    r = lax.broadcasted_iota(jnp.int32, shape, len(shape) - 2)
    c = lax.broadcasted_iota(jnp.int32, shape, len(shape) - 1)
    return r == (c % SUB)


def _peer_act_kernel(e_hbm, h_ref, gf_ref, gate_ref, tbl_ref, o_ref, act_ref, x2_ref, e_ref):
    tm = h_ref.shape[0]
    pltpu.sync_copy(e_hbm.at[:, pl.ds(pl.program_id(0) * tm, tm)], e_ref)
    n_grp = e_ref.shape[0] // PEER_GROUP
    h = h_ref[...]
    ms = jnp.sum(jnp.sum(h * h, axis=2, keepdims=True), axis=1, keepdims=True) / (h.shape[1] * h.shape[2])
    xn = (h * lax.rsqrt(ms + RMS_EPS) * gf_ref[...]).astype(BF16)
    x2_ref[...] = jnp.concatenate([xn, xn], axis=1)
    kr = lax.broadcasted_iota(jnp.int32, (PEER_GROUP, PEER_GROUP * SUB), 0)
    kc = lax.broadcasted_iota(jnp.int32, (PEER_GROUP, PEER_GROUP * SUB), 1) // SUB
    row_sum = (kr == kc).astype(BF16)
    sub = lax.broadcasted_iota(jnp.int32, (SUB, LANES), 0)

    def body(i, carry):
        qs = []
        for j in range(PEER_UNROLL):
            t = i * PEER_UNROLL + j
            xt = jnp.tile(x2_ref[t], (PEER_GROUP // 2, 1))
            qs.append(jnp.concatenate([jnp.dot(row_sum, _gather_rows(tbl_ref, e_ref, t, g) * xt,
                                               preferred_element_type=F32) for g in range(n_grp)], axis=0))
        for jb in range(0, PEER_UNROLL, SUB):
            blk = jnp.zeros((SUB, LANES), F32)
            for j in range(SUB):
                dots = jnp.sum(qs[jb + j].T, axis=0, keepdims=True)
                blk = jnp.where(sub == j, dots, blk)
            act_ref[pl.ds(pl.multiple_of(i * PEER_UNROLL + jb, SUB), SUB), :] = blk
        return carry

    lax.fori_loop(0, tm // PEER_UNROLL, body, 0)
    act = act_ref[...]
    gelu = 0.5 * act * (1.0 + lax.erf(act * (2.0 ** -0.5)))
    o_ref[...] = gate_ref[...] * gelu


def _peer_out_kernel(e_hbm, h_ref, gw_ref, tbl_ref, exp_ref, gfin_ref, o_ref, ge_ref, acc_ref, e_ref):
    tm = h_ref.shape[0]
    pltpu.sync_copy(e_hbm.at[:, pl.ds(pl.program_id(0) * tm, tm)], e_ref)
    n_grp = e_ref.shape[0] // PEER_GROUP
    ge_ref[...] = _split_dot(gw_ref[...], exp_ref[...])
    mask = _diag_mask((SUB, n_grp * LANES))

    def body(i, carry):
        for j in range(PEER_UNROLL):
            t = i * PEER_UNROLL + j
            lhs = jnp.where(mask, jnp.broadcast_to(ge_ref[pl.ds(t, 1), :], mask.shape), 0.0).astype(BF16)
            acc = jnp.zeros((SUB, LANES), F32)
            for g in range(n_grp):
                w = _gather_rows(tbl_ref, e_ref, t, g)
                acc = acc + jnp.dot(lhs[:, g * LANES:(g + 1) * LANES], w, preferred_element_type=F32)
            acc_ref[t] = acc
        return carry

    lax.fori_loop(0, tm // PEER_UNROLL, body, 0)
    y = h_ref[...] + acc_ref[...]
    ms = jnp.sum(jnp.sum(y * y, axis=2, keepdims=True), axis=1, keepdims=True) / (y.shape[1] * y.shape[2])
    o_ref[...] = y * lax.rsqrt(ms + RMS_EPS) * gfin_ref[...]


def _table_tiles(t):
    e, d = t.shape
    p = d // (2 * LANES)
    tb = t.astype(BF16).reshape(e, p, 2, LANES).transpose(0, 1, 3, 2)
    return lax.bitcast_convert_type(tb, jnp.int32).reshape(e * p, LANES)


def _peer(h2, e_idx, gate, g_ffn, g_final, u_tiles, v_tiles, tm):
    n, d = h2.shape
    kk = e_idx.shape[0]
    h3 = h2.reshape(n, SUB, d // SUB)
    row3 = pl.BlockSpec((tm, SUB, d // SUB), lambda i: (i, 0, 0))
    row = pl.BlockSpec((tm, kk), lambda i: (i, 0))
    idx = pl.BlockSpec((kk, tm), lambda i: (0, i), memory_space=pltpu.SMEM)
    lane = jnp.arange(kk * SUB) // SUB
    sel = (lane[:, None] == jnp.arange(kk)[None, :]).astype(BF16)
    gw = pl.pallas_call(
        _peer_act_kernel,
        grid=(n // tm,),
        in_specs=[pl.BlockSpec(memory_space=pl.ANY), row3, _const_spec((1, SUB, d // SUB)), row,
                  _const_spec(u_tiles.shape)],
        out_specs=row,
        out_shape=jax.ShapeDtypeStruct((n, kk), F32),
        scratch_shapes=[pltpu.VMEM((tm, kk), F32), pltpu.VMEM((tm, 2 * SUB, d // SUB), BF16),
                        pltpu.SMEM((kk, tm), jnp.int32)],
        compiler_params=_cparams("arbitrary"),
        name="peer_act",
    )(e_idx, h3, g_ffn.reshape(1, SUB, d // SUB), gate, u_tiles)
    out = pl.pallas_call(
        _peer_out_kernel,
        grid=(n // tm,),
        in_specs=[pl.BlockSpec(memory_space=pl.ANY), row3, row, _const_spec(v_tiles.shape), _const_spec(sel.T.shape),
                  _const_spec((1, SUB, d // SUB))],
        out_specs=row3,
        out_shape=jax.ShapeDtypeStruct((n, SUB, d // SUB), F32),
        scratch_shapes=[pltpu.VMEM((tm, kk * SUB), F32), pltpu.VMEM((tm, SUB, d // SUB), F32),
                        pltpu.SMEM((kk, tm), jnp.int32)],
        compiler_params=_cparams("arbitrary"),
        name="peer_out",
    )(e_idx, h3, gw, v_tiles, sel.T, g_final.reshape(1, SUB, d // SUB))
    return out.reshape(n, d)


def kernel(x, mem, positions, g_mix, w_in, attn_sinks, rwkv_mu, rwkv_w0, rwkv_w2, rwkv_a0, rwkv_a2, rwkv_g2, rwkv_k_k, rwkv_k_a, rwkv_r_k, rwkv_ln_w, rwkv_ln_b, w_out, g_cross, g_mem, w_q_cross, w_kv_cross, w_o_cross, g_ffn, peer_w_q, peer_sub_keys_1, peer_sub_keys_2, peer_u, peer_v, g_final):
    b, s, d = x.shape
    n = b * s
    L = 0
    wq = d
    wk = ATT_KV_HEADS * HEAD_DIM
    wp = 3 * d + DECAY_LORA + AAA_LORA + GATE_LORA
    widths = (wq, wk, wk, wp, 2 * d)
    rc, rs = _rope_tables(positions)
    q, k, v, p, gates = _in_proj(x.reshape(n, d), g_mix[L], w_in[L].astype(BF16), rc, rs, widths, 256)
    attn = _swa(q, k, v, attn_sinks[L], b, s)
    rw = _rwkv(p, rwkv_mu[L], rwkv_w0[L], rwkv_w2[L], rwkv_a0[L], rwkv_a2[L], rwkv_g2[L], rwkv_k_k[L], rwkv_k_a[L],
               rwkv_r_k[L], rwkv_ln_w[L], rwkv_ln_b[L], b, s)
    m = mem.shape[1]
    kv = _mem_kv(mem.reshape(b * m, d), g_mem[L], w_kv_cross[L].astype(BF16), m)
    h2, s1, s2 = _post(x.reshape(n, d), attn, rw, gates, kv, w_out[L].astype(BF16), g_cross[L],
                       w_q_cross[L].astype(BF16), w_o_cross[L].astype(BF16), g_ffn[L], peer_w_q[L].astype(BF16),
                       peer_sub_keys_1[L], peer_sub_keys_2[L], b, s, m, 256)
    e_t, g_t = _route(s1, s2, LANES)
    kk = PEER_HEADS * PEER_TOPK
    e_idx = e_t.reshape(kk, n) * 4
    gate = g_t.reshape(kk, n).T
    out = _peer(h2, e_idx, gate, g_ffn[L], g_final, _table_tiles(peer_u[L]), _table_tiles(peer_v[L]), LANES)
    return out.reshape(b, s, d)
```

```python
import functools
import math

import jax
import jax.numpy as jnp
from jax import lax
from jax.experimental import pallas as pl
from jax.experimental.pallas import tpu as pltpu

F32 = jnp.float32
BF16 = jnp.bfloat16

RMS_EPS = 1e-5
HEAD_DIM = 64
ATT_KV_HEADS = 4
WINDOW = 128
ROT_DIM = HEAD_DIM // 4
ROPE_THETA = 500000.0
RWKV_HEAD_DIM = 64
DECAY_LORA = 64
AAA_LORA = 64
GATE_LORA = 128
RWKV_GN_EPS = 64e-5
CROSS_HEADS = 4
PEER_HEADS = 8
PEER_TOPK = 16

LANES = 128
SUBLANES = 8
VMEM_LIMIT = 56 * 1024 * 1024


def _cparams(*sem):
    return pltpu.CompilerParams(dimension_semantics=sem, vmem_limit_bytes=VMEM_LIMIT)


def _const_spec(shape):
    nd = len(shape)
    return pl.BlockSpec(shape, lambda *_: (0,) * nd, pipeline_mode=pl.Buffered(1))


def _rms(x, g):
    return x * lax.rsqrt(jnp.mean(x * x, axis=-1, keepdims=True) + RMS_EPS) * g


def _sigmoid(x):
    return 1.0 / (1.0 + jnp.exp(-x))


def _rope(t, c, s):
    w = t.shape[-1]
    rep = w // LANES
    cc = jnp.tile(c, (1, rep))
    ss = jnp.tile(s, (1, rep))
    lane = lax.broadcasted_iota(jnp.int32, t.shape, 1)
    first = (lane % HEAD_DIM) < (ROT_DIM // 2)
    partner = jnp.where(first, pltpu.roll(t, w - ROT_DIM // 2, axis=1), pltpu.roll(t, ROT_DIM // 2, axis=1))
    return t * cc + partner * ss


def _inproj_kernel(x_ref, g_ref, w_ref, c_ref, s_ref, q_ref, k_ref, v_ref, p_ref, gate_ref, *, widths):
    wq, wk, wv, wp, wg = widths
    xb = _rms(x_ref[...], g_ref[...]).astype(BF16)
    c = c_ref[...]
    s = s_ref[...]
    o = 0
    q = jnp.dot(xb, w_ref[:, o:o + wq], preferred_element_type=F32)
    q_ref[...] = _rope(q, c, s).astype(q_ref.dtype)
    o += wq
    k = jnp.dot(xb, w_ref[:, o:o + wk], preferred_element_type=F32)
    k_ref[...] = _rope(k, c, s).astype(k_ref.dtype)
    o += wk
    v_ref[...] = jnp.dot(xb, w_ref[:, o:o + wv], preferred_element_type=F32).astype(v_ref.dtype)
    o += wv
    p_ref[...] = jnp.dot(xb, w_ref[:, o:o + wp], preferred_element_type=F32)
    o += wp
    gate_ref[...] = _sigmoid(jnp.dot(xb, w_ref[:, o:o + wg], preferred_element_type=F32)).astype(gate_ref.dtype)


def _in_proj(x2, g_mix, w_in_b, rope_c, rope_s, widths, tm):
    n, d = x2.shape
    wq, wk, wv, wp, wg = widths
    row = lambda w: pl.BlockSpec((tm, w), lambda i: (i, 0))
    return pl.pallas_call(
        functools.partial(_inproj_kernel, widths=widths),
        grid=(n // tm,),
        in_specs=[row(d), _const_spec((1, d)), _const_spec(w_in_b.shape), row(LANES), row(LANES)],
        out_specs=[row(wq), row(wk), row(wv), row(wp), row(wg)],
        out_shape=[jax.ShapeDtypeStruct((n, wq), BF16), jax.ShapeDtypeStruct((n, wk), BF16),
                   jax.ShapeDtypeStruct((n, wv), BF16), jax.ShapeDtypeStruct((n, wp), F32),
                   jax.ShapeDtypeStruct((n, wg), BF16)],
        compiler_params=_cparams("parallel"),
        name="in_proj",
    )(x2, g_mix.reshape(1, d), w_in_b, rope_c, rope_s)


def _rope_tables(positions):
    half = ROT_DIM // 2
    inv_freq = 1.0 / (ROPE_THETA ** (jnp.arange(0, ROT_DIM, 2, dtype=F32) / ROT_DIM))
    ang = positions.reshape(-1).astype(F32)[:, None] * inv_freq
    cos, sin = jnp.cos(ang), jnp.sin(ang)
    n = ang.shape[0]
    pad1 = jnp.ones((n, HEAD_DIM - ROT_DIM), F32)
    pad0 = jnp.zeros((n, HEAD_DIM - ROT_DIM), F32)
    c = jnp.concatenate([cos, cos, pad1], axis=1)
    s = jnp.concatenate([-sin, sin, pad0], axis=1)
    return jnp.tile(c, (1, LANES // HEAD_DIM)), jnp.tile(s, (1, LANES // HEAD_DIM))


def _swa_kernel(sink_ref, q_ref, kp_ref, kc_ref, vp_ref, vc_ref, o_ref, *, group):
    n = pl.program_id(1)
    blk = q_ref.shape[0]
    scale = HEAD_DIM ** -0.5
    rows = group * blk
    qi = lax.broadcasted_iota(jnp.int32, (rows, 2 * blk), 0) % blk
    kj = lax.broadcasted_iota(jnp.int32, (rows, 2 * blk), 1)
    valid = (kj > qi) & (kj <= qi + WINDOW) & ((kj >= blk) | (n > 0))
    rid = lax.broadcasted_iota(jnp.int32, (rows, 1), 0) // blk
    KV = range(ATT_KV_HEADS)
    sls = [slice(kv * HEAD_DIM, (kv + 1) * HEAD_DIM) for kv in KV]
    heads = [[kv * group + g for g in range(group)] for kv in KV]
    s = []
    for kv in KV:
        kk = jnp.concatenate([kp_ref[:, sls[kv]], kc_ref[:, sls[kv]]], axis=0)
        qq = jnp.concatenate([q_ref[:, h * HEAD_DIM:(h + 1) * HEAD_DIM] for h in heads[kv]], axis=0)
        s.append(lax.dot_general(qq, kk, (((1,), (1,)), ((), ())), preferred_element_type=F32))
    p, inv = [], []
    for kv in KV:
        sc = jnp.where(valid, s[kv] * scale, -1e30)
        sink = jnp.zeros((rows, 1), F32)
        for g, h in enumerate(heads[kv]):
            sink = jnp.where(rid == g, sink_ref[h], sink)
        m = jnp.maximum(jnp.max(sc, axis=-1, keepdims=True), sink)
        e = jnp.exp(sc - m)
        inv.append(1.0 / (jnp.sum(e, axis=-1, keepdims=True) + jnp.exp(sink - m)))
        p.append(e.astype(BF16))
    for kv in KV:
        vv = jnp.concatenate([vp_ref[:, sls[kv]], vc_ref[:, sls[kv]]], axis=0)
        o = jnp.dot(p[kv], vv, preferred_element_type=F32) * inv[kv]
        for g, h in enumerate(heads[kv]):
            o_ref[:, h * HEAD_DIM:(h + 1) * HEAD_DIM] = o[g * blk:(g + 1) * blk].astype(o_ref.dtype)


def _swa(q, k, v, sinks, batch, seq):
    n, wq = q.shape
    wk = k.shape[1]
    blk = WINDOW
    nb = seq // blk
    group = (wq // HEAD_DIM) // ATT_KV_HEADS
    cur = lambda w: pl.BlockSpec((blk, w), lambda b, i: (b * nb + i, 0))
    prev = lambda w: pl.BlockSpec((blk, w), lambda b, i: (b * nb + jnp.maximum(i - 1, 0), 0))
    return pl.pallas_call(
        functools.partial(_swa_kernel, group=group),
        grid=(batch, nb),
        in_specs=[pl.BlockSpec(memory_space=pltpu.SMEM), cur(wq), prev(wk), cur(wk), prev(wk), cur(wk)],
        out_specs=cur(wq),
        out_shape=jax.ShapeDtypeStruct((n, wq), BF16),
        compiler_params=_cparams("parallel", "parallel"),
        name="swa",
    )(sinks.astype(F32), q, k, k, v, v)


RW_CHUNK = 64


def _mm(a, b):
    return jnp.dot(a.astype(BF16), b.astype(BF16), preferred_element_type=F32)


def _mm_nt(a, b):
    return lax.dot_general(a.astype(BF16), b.astype(BF16), (((1,), (1,)), ((), ())), preferred_element_type=F32)


def _mm_tn(a, b):
    return lax.dot_general(a.astype(BF16), b.astype(BF16), (((0,), (0,)), ((), ())), preferred_element_type=F32)


def _hi_lo(a):
    hi = a.astype(BF16)
    return hi, (a - hi.astype(F32)).astype(BF16)


def _split_dot(a, b01):
    hi, lo = _hi_lo(a)
    return jnp.dot(hi, b01, preferred_element_type=F32) + jnp.dot(lo, b01, preferred_element_type=F32)


def _split_dot_left(a01, b):
    hi, lo = _hi_lo(b)
    return jnp.dot(a01, hi, preferred_element_type=F32) + jnp.dot(a01, lo, preferred_element_type=F32)


def _rwkv_kernel(p_ref, mu_ref, w0_ref, w2_ref, a0_ref, a2_ref, g2_ref, kk_ref, ka_ref, rk_ref, lnw_ref, lnb_ref,
                 o_ref, last_ref, st_ref, *, width):
    c = pl.program_id(1)
    C = p_ref.shape[0]
    W = width
    N = RWKV_HEAD_DIM
    SL = 2 * N
    n_slab = W // SL

    @pl.when(c == 0)
    def _():
        last_ref[...] = jnp.zeros_like(last_ref)
        st_ref[...] = jnp.zeros_like(st_ref)

    p = p_ref[...]
    row = lax.broadcasted_iota(jnp.int32, (C, 1), 0)
    prev = jnp.where(row == 0, last_ref[...], pltpu.roll(p, 1, axis=0))
    last_ref[...] = p[C - 1:C, :]
    p = p + (prev - p) * mu_ref[...]
    r = p[:, :W]
    k = p[:, W:2 * W]
    v = p[:, 2 * W:3 * W]
    o3 = 3 * W
    dw = p[:, o3:o3 + DECAY_LORA]
    da = p[:, o3 + DECAY_LORA:o3 + DECAY_LORA + AAA_LORA]
    dg = p[:, o3 + DECAY_LORA + AAA_LORA:]

    z = -(w0_ref[...] + _mm(jnp.tanh(dw), w2_ref[...]))
    w_log = -(jnp.maximum(z, 0.0) + jnp.log(1.0 + jnp.exp(-jnp.abs(z)))) - 0.5
    lw = -jnp.exp(w_log)
    a = _sigmoid(a0_ref[...] + _mm(da, a2_ref[...]))
    g = _mm(_sigmoid(dg), g2_ref[...])

    ti = lax.broadcasted_iota(jnp.int32, (C, C), 0)
    tj = lax.broadcasted_iota(jnp.int32, (C, C), 1)
    cum = _split_dot_left((ti >= tj).astype(BF16), lw)
    e_pos = jnp.exp(cum)
    e_neg = jnp.exp(-cum)
    e_prev = jnp.exp(cum - lw)
    e_end = jnp.exp(cum[C - 1:C, :] - cum)
    p_end = e_pos[C - 1:C, :]

    li = lax.broadcasted_iota(jnp.int32, (SL, SL), 0)
    lj = lax.broadcasted_iota(jnp.int32, (SL, SL), 1)
    same = (li // N) == (lj // N)
    head_sum = same.astype(BF16)
    strict = same & ((li % N) > (lj % N))
    incl = same & ((li % N) >= (lj % N))
    lane = lax.broadcasted_iota(jnp.int32, (C, SL), 1)
    m0 = lane < N

    def stack(t):
        return jnp.concatenate([jnp.where(m0, t, 0.0), jnp.where(m0, 0.0, t)], axis=0)

    S = range(n_slab)
    sls = [slice(s * SL, (s + 1) * SL) for s in S]
    h2 = 2 * C
    bf = lambda t: t.astype(BF16)
    x = [k[:, sl] * kk_ref[:, sl] for sl in sls]
    ss = [_split_dot(x[s] * x[s], head_sum) for s in S]
    kk = [x[s] / jnp.maximum(jnp.sqrt(ss[s]), 1e-12) for s in S]
    k2 = [k[:, sl] * (1.0 + (a[:, sl] - 1.0) * ka_ref[:, sl]) for sl in sls]
    kb = [kk[s] * a[:, sls[s]] for s in S]
    AR = [bf(jnp.concatenate([stack(-kk[s] * e_prev[:, sls[s]]), stack(r[:, sls[s]] * e_pos[:, sls[s]])], axis=0))
          for s in S]
    BK = [bf(jnp.concatenate([stack(kb[s] * e_neg[:, sls[s]]), stack(k2[s] * e_neg[:, sls[s]])], axis=0)) for s in S]
    V2 = [bf(stack(v[:, sl])) for sl in sls]
    Be = [bf(stack(kb[s] * e_end[:, sls[s]])) for s in S]
    Ke = [bf(stack(k2[s] * e_end[:, sls[s]])) for s in S]
    st = [st_ref[s] for s in S]
    G = [_mm_nt(AR[s], BK[s]) for s in S]
    ST = [_mm(AR[s], st[s]) for s in S]
    A_ak = [bf(jnp.where(strict, G[s][:h2, h2:], 0.0)) for s in S]
    U = [ST[s][:h2] + _mm(A_ak[s], V2[s]) for s in S]
    Ap = [bf(jnp.where(strict, G[s][:h2, :h2], 0.0)) for s in S]
    steps = int(math.log2(C))
    for i in range(steps):
        U = [U[s] + _mm(Ap[s], U[s]) for s in S]
        if i + 1 < steps:
            Ap = [bf(_mm(Ap[s], Ap[s])) for s in S]
    Ub = [bf(U[s]) for s in S]
    A_rb = [bf(jnp.where(incl, G[s][h2:, :h2], 0.0)) for s in S]
    A_rk = [bf(jnp.where(incl, G[s][h2:, h2:], 0.0)) for s in S]
    O2 = [ST[s][h2:] + _mm(A_rb[s], Ub[s]) + _mm(A_rk[s], V2[s]) for s in S]
    for s in S:
        decay = jnp.transpose(jnp.broadcast_to(p_end[:, sls[s]], (SL, SL)))
        st_new = st[s] * decay + _mm_tn(Be[s], Ub[s]) + _mm_tn(Ke[s], V2[s])
        st_ref[s] = jnp.where(same, st_new, 0.0)
    o = [O2[s][:C] + O2[s][C:] for s in S]
    mean = [_split_dot(o[s], head_sum) * (1.0 / N) for s in S]
    dlt = [o[s] - mean[s] for s in S]
    var = [_split_dot(dlt[s] * dlt[s], head_sum) * (1.0 / N) for s in S]
    bonus = [_split_dot(r[:, sls[s]] * k2[s] * rk_ref[:, sls[s]], head_sum) * v[:, sls[s]] for s in S]
    for s in S:
        sl = sls[s]
        on = dlt[s] * lax.rsqrt(var[s] + RWKV_GN_EPS) * lnw_ref[:, sl] + lnb_ref[:, sl]
        o_ref[:, sl] = ((on + bonus[s]) * g[:, sl]).astype(o_ref.dtype)


def _rwkv(p, mu, w0, w2, a0, a2, g2, k_k, k_a, r_k, ln_w, ln_b, batch, seq):
    n, wp = p.shape
    width = w0.shape[-1]
    C = RW_CHUNK
    nc = seq // C
    vec = lambda t: t.reshape(1, -1).astype(F32)
    args = [vec(mu), vec(w0), w2.astype(BF16), vec(a0), a2.astype(BF16), g2.astype(BF16), vec(k_k), vec(k_a),
            vec(r_k), vec(ln_w), vec(ln_b)]
    return pl.pallas_call(
        functools.partial(_rwkv_kernel, width=width),
        grid=(batch, nc),
        in_specs=[pl.BlockSpec((C, wp), lambda b, c: (b * nc + c, 0))] + [_const_spec(t.shape) for t in args],
        out_specs=pl.BlockSpec((C, width), lambda b, c: (b * nc + c, 0)),
        out_shape=jax.ShapeDtypeStruct((n, width), BF16),
        scratch_shapes=[pltpu.VMEM((1, wp), F32),
                        pltpu.VMEM((width // (2 * RWKV_HEAD_DIM), 2 * RWKV_HEAD_DIM, 2 * RWKV_HEAD_DIM), F32)],
        compiler_params=_cparams("parallel", "arbitrary"),
        name="rwkv",
    )(p, *args)


def _memkv_kernel(m_ref, g_ref, w_ref, o_ref):
    o_ref[...] = jnp.dot(_rms(m_ref[...], g_ref[...]).astype(BF16), w_ref[...],
                         preferred_element_type=F32).astype(o_ref.dtype)


def _mem_kv(mem2, g_mem, w_kv_b, mem_len):
    n, d = mem2.shape
    wo = w_kv_b.shape[1]
    return pl.pallas_call(
        _memkv_kernel,
        grid=(n // mem_len,),
        in_specs=[pl.BlockSpec((mem_len, d), lambda i: (i, 0)), _const_spec((1, d)), _const_spec(w_kv_b.shape)],
        out_specs=pl.BlockSpec((mem_len, wo), lambda i: (i, 0)),
        out_shape=jax.ShapeDtypeStruct((n, wo), BF16),
        compiler_params=_cparams("parallel"),
        name="mem_kv",
    )(mem2, g_mem.reshape(1, d), w_kv_b)


def _post_kernel(x_ref, at_ref, rw_ref, gt_ref, kv_ref, wo_ref, gc_ref, wq_ref, woc_ref, gf_ref, wpq_ref,
                 k1_ref, k2_ref, h_ref, s1_ref, s2_ref):
    d = x_ref.shape[1]
    ga = gt_ref[:, :d].astype(F32)
    gb = gt_ref[:, d:].astype(F32)
    mixed = ga * at_ref[...].astype(F32) + gb * rw_ref[...].astype(F32)
    h1 = x_ref[...] + jnp.dot(mixed.astype(BF16), wo_ref[...], preferred_element_type=F32)

    qc = jnp.dot(_rms(h1, gc_ref[...]).astype(BF16), wq_ref[...], preferred_element_type=F32)
    wc = qc.shape[1]
    hd = wc // CROSS_HEADS
    scale = hd ** -0.5
    H = range(CROSS_HEADS)
    qb = (qc * scale).astype(BF16)
    s = [lax.dot_general(qb[:, hh * hd:(hh + 1) * hd], kv_ref[:, hh * hd:(hh + 1) * hd],
                         (((1,), (1,)), ((), ())), preferred_element_type=F32) for hh in H]
    ex = [jnp.exp(s[hh] - jnp.max(s[hh], axis=-1, keepdims=True)) for hh in H]
    inv = [1.0 / jnp.sum(ex[hh], axis=-1, keepdims=True) for hh in H]
    outs = [jnp.dot(ex[hh].astype(BF16), kv_ref[:, wc + hh * hd:wc + (hh + 1) * hd],
                    preferred_element_type=F32) * inv[hh] for hh in H]
    oc = jnp.concatenate(outs, axis=1)
    h2 = h1 + jnp.dot(oc.astype(BF16), woc_ref[...], preferred_element_type=F32)
    h_ref[...] = h2

    q3 = jnp.dot(_rms(h2, gf_ref[...]).astype(BF16), wpq_ref[...], preferred_element_type=F32)
    half = k1_ref.shape[1]
    dn = (((1,), (1,)), ((), ()))
    q_hi, q_lo = _hi_lo(q3)
    k_parts = [_hi_lo(k1_ref[...]), _hi_lo(k2_ref[...])]

    def score(kp, c0):
        qh, ql = q_hi[:, c0:c0 + half], q_lo[:, c0:c0 + half]
        return (lax.dot_general(kp[0], qh, dn, preferred_element_type=F32)
                + lax.dot_general(kp[0], ql, dn, preferred_element_type=F32)
                + lax.dot_general(kp[1], qh, dn, preferred_element_type=F32))

    for hh in range(PEER_HEADS):
        s1_ref[hh] = score(k_parts[0], 2 * hh * half)
        s2_ref[hh] = score(k_parts[1], (2 * hh + 1) * half)


def _post(x2, attn, rw, gates, kv, w_out_b, g_cross, w_qc_b, w_oc_b, g_ffn, w_pq_b, k1, k2, batch, seq, mem_len, tm):
    n, d = x2.shape
    nt = seq // tm
    n_keys = k1.shape[0]
    row = lambda w: pl.BlockSpec((tm, w), lambda b, i: (b * nt + i, 0))
    sc_spec = pl.BlockSpec((PEER_HEADS, n_keys, tm), lambda b, i: (0, 0, b * nt + i))
    consts = [w_out_b, g_cross.reshape(1, d), w_qc_b, w_oc_b, g_ffn.reshape(1, d), w_pq_b, k1.astype(F32), k2.astype(F32)]
    return pl.pallas_call(
        _post_kernel,
        grid=(batch, nt),
        in_specs=[row(d), row(d), row(d), row(2 * d), pl.BlockSpec((mem_len, kv.shape[1]), lambda b, i: (b, 0))]
                 + [_const_spec(t.shape) for t in consts],
        out_specs=[row(d), sc_spec, sc_spec],
        out_shape=[jax.ShapeDtypeStruct((n, d), F32), jax.ShapeDtypeStruct((PEER_HEADS, n_keys, n), F32),
                   jax.ShapeDtypeStruct((PEER_HEADS, n_keys, n), F32)],
        compiler_params=_cparams("parallel", "parallel"),
        name="post",
    )(x2, attn, rw, gates, kv, *consts)


ROUTE_UNROLL = 4


def _topk_rows(s, k, payload=None):
    rows = s.shape[0]
    rid = lax.broadcasted_iota(jnp.int32, s.shape, 0).astype(F32)
    vals, outs = [], []
    for _ in range(k):
        m = jnp.max(s, axis=0, keepdims=True)
        i = jnp.min(jnp.where(s == m, rid, float(rows)), axis=0, keepdims=True)
        hit = rid == i
        vals.append(m)
        outs.append(i if payload is None else jnp.max(jnp.where(hit, payload, -1.0), axis=0, keepdims=True))
        s = jnp.where(hit, -jnp.inf, s)
    return jnp.concatenate(vals, axis=0), jnp.concatenate(outs, axis=0)


def _route_kernel(s1_ref, s2_ref, e_ref, g_ref):
    K = PEER_TOPK
    n_keys = s1_ref.shape[1]

    def one_head(h):
        v1, i1 = _topk_rows(s1_ref[h], K)
        v2, i2 = _topk_rows(s2_ref[h], K)
        cand, ecand = [], []
        for a in range(K):
            nb = K // (a + 1)
            cand.append(v1[a:a + 1, :] + v2[:nb, :])
            ecand.append(i1[a:a + 1, :] * float(n_keys) + i2[:nb, :])
        n_c = sum(c.shape[0] for c in cand)
        pad = -n_c % SUBLANES
        if pad:
            cand.append(jnp.full((pad, v1.shape[1]), -jnp.inf, F32))
            ecand.append(jnp.full((pad, v1.shape[1]), -1.0, F32))
        sc, e = _topk_rows(jnp.concatenate(cand, axis=0), K, jnp.concatenate(ecand, axis=0))
        e_ref[h] = e.astype(jnp.int32)
        ex = jnp.exp(sc - sc[0:1, :])
        g_ref[h] = ex / jnp.sum(ex, axis=0, keepdims=True)

    def body(i, carry):
        for j in range(ROUTE_UNROLL):
            one_head(i * ROUTE_UNROLL + j)
        return carry

    lax.fori_loop(0, s1_ref.shape[0] // ROUTE_UNROLL, body, 0)


def _route(s1, s2, tk):
    heads, n_keys, n = s1.shape
    spec_in = pl.BlockSpec((heads, n_keys, tk), lambda i: (0, 0, i))
    spec_out = pl.BlockSpec((heads, PEER_TOPK, tk), lambda i: (0, 0, i))
    return pl.pallas_call(
        _route_kernel,
        grid=(n // tk,),
        in_specs=[spec_in, spec_in],
        out_specs=[spec_out, spec_out],
        out_shape=[jax.ShapeDtypeStruct((heads, PEER_TOPK, n), jnp.int32),
                   jax.ShapeDtypeStruct((heads, PEER_TOPK, n), F32)],
        compiler_params=_cparams("parallel"),
        name="route",
    )(s1, s2)


PEER_GROUP = 16
SUB = SUBLANES
PEER_UNROLL = 16


def _gather_rows(tbl_ref, e_ref, t, g):
    parts = [tbl_ref[pl.ds(pl.multiple_of(e_ref.at[g * PEER_GROUP + j][t], 4), 4), :] for j in range(PEER_GROUP)]
    return pltpu.bitcast(jnp.concatenate(parts, axis=0), BF16)


def _diag_mask(shape):
    r = lax.broadcasted_iota(jnp.int32, shape, len(shape) - 2)
    c = lax.broadcasted_iota(jnp.int32, shape, len(shape) - 1)
    return r == (c % SUB)


def _with_index_tile(e_hbm, bufs, sems, tm, fn):
    i = pl.program_id(0)

    def copy(step, slot):
        return pltpu.make_async_copy(e_hbm.at[:, pl.ds(pl.multiple_of(step * tm, tm), tm)], bufs[slot], sems.at[slot])

    @pl.when(i == 0)
    def _():
        copy(0, 0).start()

    for slot in range(2):
        @pl.when(i % 2 == slot)
        def _(slot=slot):
            @pl.when(i + 1 < pl.num_programs(0))
            def _():
                copy(i + 1, 1 - slot).start()

            copy(i, slot).wait()
            fn(bufs[slot])


def _peer_act_kernel(e_hbm, h_ref, gf_ref, gate_ref, tbl_ref, o_ref, act_ref, x2_ref, e_a, e_b, sems):
    tm = h_ref.shape[0]
    n_grp = e_a.shape[0] // PEER_GROUP
    h = h_ref[...]
    ms = jnp.sum(jnp.sum(h * h, axis=2, keepdims=True), axis=1, keepdims=True) / (h.shape[1] * h.shape[2])
    xn = (h * lax.rsqrt(ms + RMS_EPS) * gf_ref[...]).astype(BF16)
    x2_ref[...] = jnp.concatenate([xn, xn], axis=1)
    kr = lax.broadcasted_iota(jnp.int32, (PEER_GROUP, PEER_GROUP * SUB), 0)
    kc = lax.broadcasted_iota(jnp.int32, (PEER_GROUP, PEER_GROUP * SUB), 1) // SUB
    row_sum = (kr == kc).astype(BF16)
    sub = lax.broadcasted_iota(jnp.int32, (SUB, LANES), 0)

    def gather_loop(e_ref):
        def body(i, carry):
            qs = []
            for j in range(PEER_UNROLL):
                t = i * PEER_UNROLL + j
                xt = jnp.tile(x2_ref[t], (PEER_GROUP // 2, 1))
                qs.append(jnp.concatenate([jnp.dot(row_sum, _gather_rows(tbl_ref, e_ref, t, g) * xt,
                                                   preferred_element_type=F32) for g in range(n_grp)], axis=0))
            for jb in range(0, PEER_UNROLL, SUB):
                blk = jnp.zeros((SUB, LANES), F32)
                for j in range(SUB):
                    dots = jnp.sum(qs[jb + j].T, axis=0, keepdims=True)
                    blk = jnp.where(sub == j, dots, blk)
                act_ref[pl.ds(pl.multiple_of(i * PEER_UNROLL + jb, SUB), SUB), :] = blk
            return carry

        lax.fori_loop(0, tm // PEER_UNROLL, body, 0)

    _with_index_tile(e_hbm, (e_a, e_b), sems, tm, gather_loop)
    act = act_ref[...]
    gelu = 0.5 * act * (1.0 + lax.erf(act * (2.0 ** -0.5)))
    o_ref[...] = gate_ref[...] * gelu


def _peer_out_kernel(e_hbm, h_ref, gw_ref, tbl_ref, exp_ref, gfin_ref, o_ref, ge_ref, acc_ref, e_a, e_b, sems):
    tm = h_ref.shape[0]
    n_grp = e_a.shape[0] // PEER_GROUP
    ge_ref[...] = _split_dot(gw_ref[...], exp_ref[...])
    mask = _diag_mask((SUB, n_grp * LANES))

    def gather_loop(e_ref):
        def body(i, carry):
            for j in range(PEER_UNROLL):
                t = i * PEER_UNROLL + j
                lhs = jnp.where(mask, jnp.broadcast_to(ge_ref[pl.ds(t, 1), :], mask.shape), 0.0).astype(BF16)
                acc = jnp.zeros((SUB, LANES), F32)
                for g in range(n_grp):
                    w = _gather_rows(tbl_ref, e_ref, t, g)
                    acc = acc + jnp.dot(lhs[:, g * LANES:(g + 1) * LANES], w, preferred_element_type=F32)
                acc_ref[t] = acc
            return carry

        lax.fori_loop(0, tm // PEER_UNROLL, body, 0)

    _with_index_tile(e_hbm, (e_a, e_b), sems, tm, gather_loop)
    y = h_ref[...] + acc_ref[...]
    ms = jnp.sum(jnp.sum(y * y, axis=2, keepdims=True), axis=1, keepdims=True) / (y.shape[1] * y.shape[2])
    o_ref[...] = y * lax.rsqrt(ms + RMS_EPS) * gfin_ref[...]


def _table_tiles(t):
    e, d = t.shape
    p = d // (2 * LANES)
    tb = t.astype(BF16).reshape(e, p, 2, LANES).transpose(0, 1, 3, 2)
    return lax.bitcast_convert_type(tb, jnp.int32).reshape(e * p, LANES)


def _peer(h2, e_idx, gate, g_ffn, g_final, u_tiles, v_tiles, tm):
    n, d = h2.shape
    kk = e_idx.shape[0]
    h3 = h2.reshape(n, SUB, d // SUB)
    row3 = pl.BlockSpec((tm, SUB, d // SUB), lambda i: (i, 0, 0))
    row = pl.BlockSpec((tm, kk), lambda i: (i, 0))
    idx = pl.BlockSpec((kk, tm), lambda i: (0, i), memory_space=pltpu.SMEM)
    lane = jnp.arange(kk * SUB) // SUB
    sel = (lane[:, None] == jnp.arange(kk)[None, :]).astype(BF16)
    gw = pl.pallas_call(
        _peer_act_kernel,
        grid=(n // tm,),
        in_specs=[pl.BlockSpec(memory_space=pl.ANY), row3, _const_spec((1, SUB, d // SUB)), row,
                  _const_spec(u_tiles.shape)],
        out_specs=row,
        out_shape=jax.ShapeDtypeStruct((n, kk), F32),
        scratch_shapes=[pltpu.VMEM((tm, kk), F32), pltpu.VMEM((tm, 2 * SUB, d // SUB), BF16),
                        pltpu.SMEM((kk, tm), jnp.int32), pltpu.SMEM((kk, tm), jnp.int32),
                        pltpu.SemaphoreType.DMA((2,))],
        compiler_params=_cparams("arbitrary"),
        name="peer_act",
    )(e_idx, h3, g_ffn.reshape(1, SUB, d // SUB), gate, u_tiles)
    out = pl.pallas_call(
        _peer_out_kernel,
        grid=(n // tm,),
        in_specs=[pl.BlockSpec(memory_space=pl.ANY), row3, row, _const_spec(v_tiles.shape), _const_spec(sel.T.shape),
                  _const_spec((1, SUB, d // SUB))],
        out_specs=row3,
        out_shape=jax.ShapeDtypeStruct((n, SUB, d // SUB), F32),
        scratch_shapes=[pltpu.VMEM((tm, kk * SUB), F32), pltpu.VMEM((tm, SUB, d // SUB), F32),
                        pltpu.SMEM((kk, tm), jnp.int32), pltpu.SMEM((kk, tm), jnp.int32),
                        pltpu.SemaphoreType.DMA((2,))],
        compiler_params=_cparams("arbitrary"),
        name="peer_out",
    )(e_idx, h3, gw, v_tiles, sel.T, g_final.reshape(1, SUB, d // SUB))
    return out.reshape(n, d)


def kernel(x, mem, positions, g_mix, w_in, attn_sinks, rwkv_mu, rwkv_w0, rwkv_w2, rwkv_a0, rwkv_a2, rwkv_g2, rwkv_k_k, rwkv_k_a, rwkv_r_k, rwkv_ln_w, rwkv_ln_b, w_out, g_cross, g_mem, w_q_cross, w_kv_cross, w_o_cross, g_ffn, peer_w_q, peer_sub_keys_1, peer_sub_keys_2, peer_u, peer_v, g_final):
    b, s, d = x.shape
    n = b * s
    L = 0
    wq = d
    wk = ATT_KV_HEADS * HEAD_DIM
    wp = 3 * d + DECAY_LORA + AAA_LORA + GATE_LORA
    widths = (wq, wk, wk, wp, 2 * d)
    rc, rs = _rope_tables(positions)
    q, k, v, p, gates = _in_proj(x.reshape(n, d), g_mix[L], w_in[L].astype(BF16), rc, rs, widths, 256)
    attn = _swa(q, k, v, attn_sinks[L], b, s)
    rw = _rwkv(p, rwkv_mu[L], rwkv_w0[L], rwkv_w2[L], rwkv_a0[L], rwkv_a2[L], rwkv_g2[L], rwkv_k_k[L], rwkv_k_a[L],
               rwkv_r_k[L], rwkv_ln_w[L], rwkv_ln_b[L], b, s)
    m = mem.shape[1]
    kv = _mem_kv(mem.reshape(b * m, d), g_mem[L], w_kv_cross[L].astype(BF16), m)
    h2, s1, s2 = _post(x.reshape(n, d), attn, rw, gates, kv, w_out[L].astype(BF16), g_cross[L],
                       w_q_cross[L].astype(BF16), w_o_cross[L].astype(BF16), g_ffn[L], peer_w_q[L].astype(BF16),
                       peer_sub_keys_1[L], peer_sub_keys_2[L], b, s, m, 256)
    e_t, g_t = _route(s1, s2, LANES)
    kk = PEER_HEADS * PEER_TOPK
    e_idx = e_t.reshape(kk, n) * 4
    gate = g_t.reshape(kk, n).T
    out = _peer(h2, e_idx, gate, g_ffn[L], g_final, _table_tiles(peer_u[L]), _table_tiles(peer_v[L]), LANES)
    return out.reshape(b, s, d)
```

```python
import functools
import math

import jax
import jax.numpy as jnp
from jax import lax
from jax.experimental import pallas as pl
from jax.experimental.pallas import tpu as pltpu

F32 = jnp.float32
BF16 = jnp.bfloat16

RMS_EPS = 1e-5
HEAD_DIM = 64
ATT_KV_HEADS = 4
WINDOW = 128
ROT_DIM = HEAD_DIM // 4
ROPE_THETA = 500000.0
RWKV_HEAD_DIM = 64
DECAY_LORA = 64
AAA_LORA = 64
GATE_LORA = 128
RWKV_GN_EPS = 64e-5
CROSS_HEADS = 4
PEER_HEADS = 8
PEER_TOPK = 16

LANES = 128
SUBLANES = 8
VMEM_LIMIT = 56 * 1024 * 1024


def _cparams(*sem):
    return pltpu.CompilerParams(dimension_semantics=sem, vmem_limit_bytes=VMEM_LIMIT)


def _const_spec(shape):
    nd = len(shape)
    return pl.BlockSpec(shape, lambda *_: (0,) * nd, pipeline_mode=pl.Buffered(1))


def _rms(x, g):
    return x * lax.rsqrt(jnp.mean(x * x, axis=-1, keepdims=True) + RMS_EPS) * g


def _sigmoid(x):
    return 1.0 / (1.0 + jnp.exp(-x))


def _rope(t, c, s):
    w = t.shape[-1]
    rep = w // LANES
    cc = jnp.tile(c, (1, rep))
    ss = jnp.tile(s, (1, rep))
    lane = lax.broadcasted_iota(jnp.int32, t.shape, 1)
    first = (lane % HEAD_DIM) < (ROT_DIM // 2)
    partner = jnp.where(first, pltpu.roll(t, w - ROT_DIM // 2, axis=1), pltpu.roll(t, ROT_DIM // 2, axis=1))
    return t * cc + partner * ss


def _inproj_kernel(x_ref, g_ref, w_ref, c_ref, s_ref, q_ref, k_ref, v_ref, p_ref, gate_ref, *, widths):
    wq, wk, wv, wp, wg = widths
    xb = _rms(x_ref[...], g_ref[...]).astype(BF16)
    c = c_ref[...]
    s = s_ref[...]
    o = 0
    q = jnp.dot(xb, w_ref[:, o:o + wq], preferred_element_type=F32)
    q_ref[...] = _rope(q, c, s).astype(q_ref.dtype)
    o += wq
    k = jnp.dot(xb, w_ref[:, o:o + wk], preferred_element_type=F32)
    k_ref[...] = _rope(k, c, s).astype(k_ref.dtype)
    o += wk
    v_ref[...] = jnp.dot(xb, w_ref[:, o:o + wv], preferred_element_type=F32).astype(v_ref.dtype)
    o += wv
    p_ref[...] = jnp.dot(xb, w_ref[:, o:o + wp], preferred_element_type=F32)
    o += wp
    gate_ref[...] = _sigmoid(jnp.dot(xb, w_ref[:, o:o + wg], preferred_element_type=F32)).astype(gate_ref.dtype)


def _in_proj(x2, g_mix, w_in_b, rope_c, rope_s, widths, tm):
    n, d = x2.shape
    wq, wk, wv, wp, wg = widths
    row = lambda w: pl.BlockSpec((tm, w), lambda i: (i, 0))
    return pl.pallas_call(
        functools.partial(_inproj_kernel, widths=widths),
        grid=(n // tm,),
        in_specs=[row(d), _const_spec((1, d)), _const_spec(w_in_b.shape), row(LANES), row(LANES)],
        out_specs=[row(wq), row(wk), row(wv), row(wp), row(wg)],
        out_shape=[jax.ShapeDtypeStruct((n, wq), BF16), jax.ShapeDtypeStruct((n, wk), BF16),
                   jax.ShapeDtypeStruct((n, wv), BF16), jax.ShapeDtypeStruct((n, wp), F32),
                   jax.ShapeDtypeStruct((n, wg), BF16)],
        compiler_params=_cparams("parallel"),
        name="in_proj",
    )(x2, g_mix.reshape(1, d), w_in_b, rope_c, rope_s)


def _rope_tables(positions):
    half = ROT_DIM // 2
    inv_freq = 1.0 / (ROPE_THETA ** (jnp.arange(0, ROT_DIM, 2, dtype=F32) / ROT_DIM))
    ang = positions.reshape(-1).astype(F32)[:, None] * inv_freq
    cos, sin = jnp.cos(ang), jnp.sin(ang)
    n = ang.shape[0]
    pad1 = jnp.ones((n, HEAD_DIM - ROT_DIM), F32)
    pad0 = jnp.zeros((n, HEAD_DIM - ROT_DIM), F32)
    c = jnp.concatenate([cos, cos, pad1], axis=1)
    s = jnp.concatenate([-sin, sin, pad0], axis=1)
    return jnp.tile(c, (1, LANES // HEAD_DIM)), jnp.tile(s, (1, LANES // HEAD_DIM))


def _swa_kernel(sink_ref, q_ref, kp_ref, kc_ref, vp_ref, vc_ref, o_ref, *, group):
    n = pl.program_id(1)
    blk = q_ref.shape[0]
    scale = HEAD_DIM ** -0.5
    rows = group * blk
    qi = lax.broadcasted_iota(jnp.int32, (rows, 2 * blk), 0) % blk
    kj = lax.broadcasted_iota(jnp.int32, (rows, 2 * blk), 1)
    valid = (kj > qi) & (kj <= qi + WINDOW) & ((kj >= blk) | (n > 0))
    rid = lax.broadcasted_iota(jnp.int32, (rows, 1), 0) // blk
    KV = range(ATT_KV_HEADS)
    sls = [slice(kv * HEAD_DIM, (kv + 1) * HEAD_DIM) for kv in KV]
    heads = [[kv * group + g for g in range(group)] for kv in KV]
    s = []
    for kv in KV:
        kk = jnp.concatenate([kp_ref[:, sls[kv]], kc_ref[:, sls[kv]]], axis=0)
        qq = jnp.concatenate([q_ref[:, h * HEAD_DIM:(h + 1) * HEAD_DIM] for h in heads[kv]], axis=0)
        s.append(lax.dot_general(qq, kk, (((1,), (1,)), ((), ())), preferred_element_type=F32))
    p, inv = [], []
    for kv in KV:
        sc = jnp.where(valid, s[kv] * scale, -1e30)
        sink = jnp.zeros((rows, 1), F32)
        for g, h in enumerate(heads[kv]):
            sink = jnp.where(rid == g, sink_ref[h], sink)
        m = jnp.maximum(jnp.max(sc, axis=-1, keepdims=True), sink)
        e = jnp.exp(sc - m)
        inv.append(1.0 / (jnp.sum(e, axis=-1, keepdims=True) + jnp.exp(sink - m)))
        p.append(e.astype(BF16))
    for kv in KV:
        vv = jnp.concatenate([vp_ref[:, sls[kv]], vc_ref[:, sls[kv]]], axis=0)
        o = jnp.dot(p[kv], vv, preferred_element_type=F32) * inv[kv]
        for g, h in enumerate(heads[kv]):
            o_ref[:, h * HEAD_DIM:(h + 1) * HEAD_DIM] = o[g * blk:(g + 1) * blk].astype(o_ref.dtype)


def _swa(q, k, v, sinks, batch, seq):
    n, wq = q.shape
    wk = k.shape[1]
    blk = WINDOW
    nb = seq // blk
    group = (wq // HEAD_DIM) // ATT_KV_HEADS
    cur = lambda w: pl.BlockSpec((blk, w), lambda b, i: (b * nb + i, 0))
    prev = lambda w: pl.BlockSpec((blk, w), lambda b, i: (b * nb + jnp.maximum(i - 1, 0), 0))
    return pl.pallas_call(
        functools.partial(_swa_kernel, group=group),
        grid=(batch, nb),
        in_specs=[pl.BlockSpec(memory_space=pltpu.SMEM), cur(wq), prev(wk), cur(wk), prev(wk), cur(wk)],
        out_specs=cur(wq),
        out_shape=jax.ShapeDtypeStruct((n, wq), BF16),
        compiler_params=_cparams("parallel", "parallel"),
        name="swa",
    )(sinks.astype(F32), q, k, k, v, v)


RW_CHUNK = 64
RW_SEQS = 2


def _mm(a, b):
    return jnp.dot(a.astype(BF16), b.astype(BF16), preferred_element_type=F32)


def _mm_nt(a, b):
    return lax.dot_general(a.astype(BF16), b.astype(BF16), (((1,), (1,)), ((), ())), preferred_element_type=F32)


def _mm_tn(a, b):
    return lax.dot_general(a.astype(BF16), b.astype(BF16), (((0,), (0,)), ((), ())), preferred_element_type=F32)


def _hi_lo(a):
    hi = a.astype(BF16)
    return hi, (a - hi.astype(F32)).astype(BF16)


def _split_dot(a, b01):
    hi, lo = _hi_lo(a)
    return jnp.dot(hi, b01, preferred_element_type=F32) + jnp.dot(lo, b01, preferred_element_type=F32)


def _split_dot_left(a01, b):
    hi, lo = _hi_lo(b)
    return jnp.dot(a01, hi, preferred_element_type=F32) + jnp.dot(a01, lo, preferred_element_type=F32)


def _rwkv_kernel(p_ref, mu_ref, w0_ref, w2_ref, a0_ref, a2_ref, g2_ref, kk_ref, ka_ref, rk_ref, lnw_ref, lnb_ref,
                 o_ref, last_ref, st_ref, *, width):
    c = pl.program_id(1)
    NB, C, wp = p_ref.shape
    R = NB * C
    W = width
    N = RWKV_HEAD_DIM
    SL = 2 * N
    n_slab = W // SL

    @pl.when(c == 0)
    def _():
        last_ref[...] = jnp.zeros_like(last_ref)
        st_ref[...] = jnp.zeros_like(st_ref)

    p = p_ref[...].reshape(R, wp)
    row = lax.broadcasted_iota(jnp.int32, (R, 1), 0)
    carry = jnp.concatenate([jnp.broadcast_to(last_ref[b:b + 1, :], (C, wp)) for b in range(NB)], axis=0)
    prev = jnp.where(row % C == 0, carry, pltpu.roll(p, 1, axis=0))
    for b in range(NB):
        last_ref[b:b + 1, :] = p[(b + 1) * C - 1:(b + 1) * C, :]
    p = p + (prev - p) * mu_ref[...]
    r = p[:, :W]
    k = p[:, W:2 * W]
    v = p[:, 2 * W:3 * W]
    o3 = 3 * W
    dw = p[:, o3:o3 + DECAY_LORA]
    da = p[:, o3 + DECAY_LORA:o3 + DECAY_LORA + AAA_LORA]
    dg = p[:, o3 + DECAY_LORA + AAA_LORA:]

    z = -(w0_ref[...] + _mm(jnp.tanh(dw), w2_ref[...]))
    w_log = -(jnp.maximum(z, 0.0) + jnp.log(1.0 + jnp.exp(-jnp.abs(z)))) - 0.5
    lw = -jnp.exp(w_log)
    a = _sigmoid(a0_ref[...] + _mm(da, a2_ref[...]))
    g = _mm(_sigmoid(dg), g2_ref[...])

    ti = lax.broadcasted_iota(jnp.int32, (R, R), 0)
    tj = lax.broadcasted_iota(jnp.int32, (R, R), 1)
    tri = ((ti >= tj) & (ti // C == tj // C)).astype(BF16)
    cum = _split_dot_left(tri, lw)
    e_pos = jnp.exp(cum)
    e_neg = jnp.exp(-cum)
    e_prev = jnp.exp(cum - lw)

    li = lax.broadcasted_iota(jnp.int32, (SL, SL), 0)
    lj = lax.broadcasted_iota(jnp.int32, (SL, SL), 1)
    same = (li // N) == (lj // N)
    head_sum = same.astype(BF16)
    strict = same & ((li % N) > (lj % N))
    incl = same & ((li % N) >= (lj % N))
    lane = lax.broadcasted_iota(jnp.int32, (C, SL), 1)
    m0 = lane < N

    def stack(t):
        return jnp.concatenate([jnp.where(m0, t, 0.0), jnp.where(m0, 0.0, t)], axis=0)

    items = [(b, s) for b in range(NB) for s in range(n_slab)]
    S = range(len(items))
    at = [(slice(b * C, (b + 1) * C), slice(s * SL, (s + 1) * SL)) for b, s in items]
    lanes = [sl for _, sl in at]
    h2 = 2 * C
    bf = lambda t: t.astype(BF16)
    e_end = [jnp.exp(cum[rw.stop - 1:rw.stop, sl] - cum[rw, sl]) for rw, sl in at]
    x = [k[i] * kk_ref[:, i[1]] for i in at]
    k2 = [k[i] * (1.0 + (a[i] - 1.0) * ka_ref[:, i[1]]) for i in at]
    sums = [_split_dot(jnp.concatenate([x[s] * x[s], r[at[s]] * k2[s] * rk_ref[:, lanes[s]]], axis=0), head_sum)
            for s in S]
    kk = [x[s] / jnp.maximum(jnp.sqrt(sums[s][:C]), 1e-12) for s in S]
    kb = [kk[s] * a[at[s]] for s in S]
    AR = [bf(jnp.concatenate([stack(-kk[s] * e_prev[at[s]]), stack(r[at[s]] * e_pos[at[s]])], axis=0))
          for s in S]
    twice = lambda t: jnp.concatenate([t, t], axis=0)
    BK = [bf(jnp.concatenate([twice(kb[s] * e_neg[at[s]]), twice(k2[s] * e_neg[at[s]])], axis=0)) for s in S]
    V2 = [bf(stack(v[i])) for i in at]
    Be = [bf(twice(kb[s] * e_end[s])) for s in S]
    Ke = [bf(twice(k2[s] * e_end[s])) for s in S]
    st = [st_ref[s] for s in S]
    G = [_mm_nt(AR[s], BK[s]) for s in S]
    ST = [_mm(AR[s], st[s]) for s in S]
    A_k = [bf(jnp.concatenate([jnp.where(strict, G[s][:h2, h2:], 0.0), jnp.where(incl, G[s][h2:, h2:], 0.0)],
                              axis=0)) for s in S]
    AV = [_mm(A_k[s], V2[s]) for s in S]
    U = [ST[s][:h2] + AV[s][:h2] for s in S]
    Ap = [bf(jnp.where(strict, G[s][:h2, :h2], 0.0)) for s in S]
    steps = int(math.log2(C))
    for i in range(steps - 1):
        Rm = [_mm(Ap[s], jnp.concatenate([Ap[s], bf(U[s])], axis=1)) for s in S]
        Ap = [bf(Rm[s][:, :SL]) for s in S]
        U = [U[s] + Rm[s][:, SL:] for s in S]
    U = [U[s] + _mm(Ap[s], U[s]) for s in S]
    Ub = [bf(U[s]) for s in S]
    A_rb = [bf(jnp.where(incl, G[s][h2:, :h2], 0.0)) for s in S]
    O2 = [ST[s][h2:] + AV[s][h2:] + _mm(A_rb[s], Ub[s]) for s in S]
    for s in S:
        rw, sl = at[s]
        decay = jnp.transpose(jnp.broadcast_to(e_pos[rw.stop - 1:rw.stop, sl], (SL, SL)))
        st_new = st[s] * decay + _mm_tn(Be[s], Ub[s]) + _mm_tn(Ke[s], V2[s])
        st_ref[s] = jnp.where(same, st_new, 0.0)
    o = [O2[s][:C] + O2[s][C:] for s in S]
    mean = [_split_dot(o[s], head_sum) * (1.0 / N) for s in S]
    dlt = [o[s] - mean[s] for s in S]
    var = [_split_dot(dlt[s] * dlt[s], head_sum) * (1.0 / N) for s in S]
    bonus = [sums[s][C:] * v[at[s]] for s in S]
    for s in S:
        (b, _), sl = items[s], lanes[s]
        on = dlt[s] * lax.rsqrt(var[s] + RWKV_GN_EPS) * lnw_ref[:, sl] + lnb_ref[:, sl]
        o_ref[b, :, sl] = ((on + bonus[s]) * g[at[s]]).astype(o_ref.dtype)


def _rwkv(p, mu, w0, w2, a0, a2, g2, k_k, k_a, r_k, ln_w, ln_b, batch, seq):
    n, wp = p.shape
    width = w0.shape[-1]
    C = RW_CHUNK
    nb = RW_SEQS
    vec = lambda t: t.reshape(1, -1).astype(F32)
    args = [vec(mu), vec(w0), w2.astype(BF16), vec(a0), a2.astype(BF16), g2.astype(BF16), vec(k_k), vec(k_a),
            vec(r_k), vec(ln_w), vec(ln_b)]
    out = pl.pallas_call(
        functools.partial(_rwkv_kernel, width=width),
        grid=(batch // nb, seq // C),
        in_specs=[pl.BlockSpec((nb, C, wp), lambda b, c: (b, c, 0))] + [_const_spec(t.shape) for t in args],
        out_specs=pl.BlockSpec((nb, C, width), lambda b, c: (b, c, 0)),
        out_shape=jax.ShapeDtypeStruct((batch, seq, width), BF16),
        scratch_shapes=[pltpu.VMEM((nb, wp), F32),
                        pltpu.VMEM((nb * width // (2 * RWKV_HEAD_DIM), 2 * RWKV_HEAD_DIM, 2 * RWKV_HEAD_DIM), F32)],
        compiler_params=_cparams("parallel", "arbitrary"),
        name="rwkv",
    )(p.reshape(batch, seq, wp), *args)
    return out.reshape(n, width)


def _memkv_kernel(m_ref, g_ref, w_ref, o_ref):
    o_ref[...] = jnp.dot(_rms(m_ref[...], g_ref[...]).astype(BF16), w_ref[...],
                         preferred_element_type=F32).astype(o_ref.dtype)


def _mem_kv(mem2, g_mem, w_kv_b, mem_len):
    n, d = mem2.shape
    wo = w_kv_b.shape[1]
    return pl.pallas_call(
        _memkv_kernel,
        grid=(n // mem_len,),
        in_specs=[pl.BlockSpec((mem_len, d), lambda i: (i, 0)), _const_spec((1, d)), _const_spec(w_kv_b.shape)],
        out_specs=pl.BlockSpec((mem_len, wo), lambda i: (i, 0)),
        out_shape=jax.ShapeDtypeStruct((n, wo), BF16),
        compiler_params=_cparams("parallel"),
        name="mem_kv",
    )(mem2, g_mem.reshape(1, d), w_kv_b)


def _post_kernel(x_ref, at_ref, rw_ref, gt_ref, kv_ref, wo_ref, gc_ref, wq_ref, woc_ref, gf_ref, wpq_ref,
                 k1_ref, k2_ref, h_ref, s1_ref, s2_ref):
    d = x_ref.shape[1]
    ga = gt_ref[:, :d].astype(F32)
    gb = gt_ref[:, d:].astype(F32)
    mixed = ga * at_ref[...].astype(F32) + gb * rw_ref[...].astype(F32)
    h1 = x_ref[...] + jnp.dot(mixed.astype(BF16), wo_ref[...], preferred_element_type=F32)

    qc = jnp.dot(_rms(h1, gc_ref[...]).astype(BF16), wq_ref[...], preferred_element_type=F32)
    wc = qc.shape[1]
    hd = wc // CROSS_HEADS
    scale = hd ** -0.5
    H = range(CROSS_HEADS)
    qb = (qc * scale).astype(BF16)
    s = [lax.dot_general(qb[:, hh * hd:(hh + 1) * hd], kv_ref[:, hh * hd:(hh + 1) * hd],
                         (((1,), (1,)), ((), ())), preferred_element_type=F32) for hh in H]
    ex = [jnp.exp(s[hh] - jnp.max(s[hh], axis=-1, keepdims=True)) for hh in H]
    inv = [1.0 / jnp.sum(ex[hh], axis=-1, keepdims=True) for hh in H]
    outs = [jnp.dot(ex[hh].astype(BF16), kv_ref[:, wc + hh * hd:wc + (hh + 1) * hd],
                    preferred_element_type=F32) * inv[hh] for hh in H]
    oc = jnp.concatenate(outs, axis=1)
    h2 = h1 + jnp.dot(oc.astype(BF16), woc_ref[...], preferred_element_type=F32)
    h_ref[...] = h2

    q3 = jnp.dot(_rms(h2, gf_ref[...]).astype(BF16), wpq_ref[...], preferred_element_type=F32)
    half = k1_ref.shape[1]
    dn = (((1,), (1,)), ((), ()))
    q_hi, q_lo = _hi_lo(q3)
    k_parts = [_hi_lo(k1_ref[...]), _hi_lo(k2_ref[...])]

    def score(kp, c0):
        qh, ql = q_hi[:, c0:c0 + half], q_lo[:, c0:c0 + half]
        return (lax.dot_general(kp[0], qh, dn, preferred_element_type=F32)
                + lax.dot_general(kp[0], ql, dn, preferred_element_type=F32)
                + lax.dot_general(kp[1], qh, dn, preferred_element_type=F32))

    for hh in range(PEER_HEADS):
        s1_ref[hh] = score(k_parts[0], 2 * hh * half)
        s2_ref[hh] = score(k_parts[1], (2 * hh + 1) * half)


def _post(x2, attn, rw, gates, kv, w_out_b, g_cross, w_qc_b, w_oc_b, g_ffn, w_pq_b, k1, k2, batch, seq, mem_len, tm):
    n, d = x2.shape
    nt = seq // tm
    n_keys = k1.shape[0]
    row = lambda w: pl.BlockSpec((tm, w), lambda b, i: (b * nt + i, 0))
    sc_spec = pl.BlockSpec((PEER_HEADS, n_keys, tm), lambda b, i: (0, 0, b * nt + i))
    consts = [w_out_b, g_cross.reshape(1, d), w_qc_b, w_oc_b, g_ffn.reshape(1, d), w_pq_b, k1.astype(F32), k2.astype(F32)]
    return pl.pallas_call(
        _post_kernel,
        grid=(batch, nt),
        in_specs=[row(d), row(d), row(d), row(2 * d), pl.BlockSpec((mem_len, kv.shape[1]), lambda b, i: (b, 0))]
                 + [_const_spec(t.shape) for t in consts],
        out_specs=[row(d), sc_spec, sc_spec],
        out_shape=[jax.ShapeDtypeStruct((n, d), F32), jax.ShapeDtypeStruct((PEER_HEADS, n_keys, n), F32),
                   jax.ShapeDtypeStruct((PEER_HEADS, n_keys, n), F32)],
        compiler_params=_cparams("parallel", "parallel"),
        name="post",
    )(x2, attn, rw, gates, kv, *consts)


ROUTE_UNROLL = 4


def _topk_rows(s, k, payload=None):
    rows = s.shape[0]
    rid = lax.broadcasted_iota(jnp.int32, s.shape, 0).astype(F32)
    vals, outs = [], []
    for _ in range(k):
        m = jnp.max(s, axis=0, keepdims=True)
        i = jnp.min(jnp.where(s == m, rid, float(rows)), axis=0, keepdims=True)
        hit = rid == i
        vals.append(m)
        outs.append(i if payload is None else jnp.max(jnp.where(hit, payload, -1.0), axis=0, keepdims=True))
        s = jnp.where(hit, -jnp.inf, s)
    return jnp.concatenate(vals, axis=0), jnp.concatenate(outs, axis=0)


def _route_kernel(s1_ref, s2_ref, e_ref, g_ref):
    K = PEER_TOPK
    n_keys = s1_ref.shape[1]

    def one_head(h):
        v1, i1 = _topk_rows(s1_ref[h], K)
        v2, i2 = _topk_rows(s2_ref[h], K)
        cand, ecand = [], []
        for a in range(K):
            nb = K // (a + 1)
            cand.append(v1[a:a + 1, :] + v2[:nb, :])
            ecand.append(i1[a:a + 1, :] * float(n_keys) + i2[:nb, :])
        n_c = sum(c.shape[0] for c in cand)
        pad = -n_c % SUBLANES
        if pad:
            cand.append(jnp.full((pad, v1.shape[1]), -jnp.inf, F32))
            ecand.append(jnp.full((pad, v1.shape[1]), -1.0, F32))
        sc, e = _topk_rows(jnp.concatenate(cand, axis=0), K, jnp.concatenate(ecand, axis=0))
        e_ref[h] = e.astype(jnp.int32)
        ex = jnp.exp(sc - sc[0:1, :])
        g_ref[h] = ex / jnp.sum(ex, axis=0, keepdims=True)

    def body(i, carry):
        for j in range(ROUTE_UNROLL):
            one_head(i * ROUTE_UNROLL + j)
        return carry

    lax.fori_loop(0, s1_ref.shape[0] // ROUTE_UNROLL, body, 0)


def _route(s1, s2, tk):
    heads, n_keys, n = s1.shape
    spec_in = pl.BlockSpec((heads, n_keys, tk), lambda i: (0, 0, i))
    spec_out = pl.BlockSpec((heads, PEER_TOPK, tk), lambda i: (0, 0, i))
    return pl.pallas_call(
        _route_kernel,
        grid=(n // tk,),
        in_specs=[spec_in, spec_in],
        out_specs=[spec_out, spec_out],
        out_shape=[jax.ShapeDtypeStruct((heads, PEER_TOPK, n), jnp.int32),
                   jax.ShapeDtypeStruct((heads, PEER_TOPK, n), F32)],
        compiler_params=_cparams("parallel"),
        name="route",
    )(s1, s2)


PEER_GROUP = 16
SUB = SUBLANES
PEER_UNROLL = 32


def _gather_rows(tbl_ref, e_ref, t, g):
    parts = [tbl_ref[pl.ds(pl.multiple_of(e_ref.at[g * PEER_GROUP + j][t], 4), 4), :] for j in range(PEER_GROUP)]
    return pltpu.bitcast(jnp.concatenate(parts, axis=0), BF16)


def _diag_mask(shape):
    r = lax.broadcasted_iota(jnp.int32, shape, len(shape) - 2)
    c = lax.broadcasted_iota(jnp.int32, shape, len(shape) - 1)
    return r == (c % SUB)


def _with_index_tile(e_hbm, bufs, sems, tm, fn):
    i = pl.program_id(0)

    def copy(step, slot):
        return pltpu.make_async_copy(e_hbm.at[:, pl.ds(pl.multiple_of(step * tm, tm), tm)], bufs[slot], sems.at[slot])

    @pl.when(i == 0)
    def _():
        copy(0, 0).start()

    for slot in range(2):
        @pl.when(i % 2 == slot)
        def _(slot=slot):
            @pl.when(i + 1 < pl.num_programs(0))
            def _():
                copy(i + 1, 1 - slot).start()

            copy(i, slot).wait()
            fn(bufs[slot])


def _peer_act_kernel(e_hbm, h_ref, gf_ref, gate_ref, tbl_ref, o_ref, act_ref, x2_ref, e_a, e_b, sems):
    tm = h_ref.shape[0]
    n_grp = e_a.shape[0] // PEER_GROUP
    h = h_ref[...]
    ms = jnp.sum(jnp.sum(h * h, axis=2, keepdims=True), axis=1, keepdims=True) / (h.shape[1] * h.shape[2])
    xn = (h * lax.rsqrt(ms + RMS_EPS) * gf_ref[...]).astype(BF16)
    x2_ref[...] = jnp.concatenate([xn, xn], axis=1)
    kr = lax.broadcasted_iota(jnp.int32, (PEER_GROUP, PEER_GROUP * SUB), 0)
    kc = lax.broadcasted_iota(jnp.int32, (PEER_GROUP, PEER_GROUP * SUB), 1) // SUB
    row_sum = (kr == kc).astype(BF16)
    sub = lax.broadcasted_iota(jnp.int32, (SUB, LANES), 0)

    def gather_loop(e_ref):
        def body(i, carry):
            qs = []
            for j in range(PEER_UNROLL):
                t = i * PEER_UNROLL + j
                xt = jnp.tile(x2_ref[t], (PEER_GROUP // 2, 1))
                qs.append(jnp.concatenate([jnp.dot(row_sum, _gather_rows(tbl_ref, e_ref, t, g) * xt,
                                                   preferred_element_type=F32) for g in range(n_grp)], axis=0))
            for jb in range(0, PEER_UNROLL, SUB):
                blk = jnp.zeros((SUB, LANES), F32)
                for j in range(SUB):
                    dots = jnp.sum(qs[jb + j].T, axis=0, keepdims=True)
                    blk = jnp.where(sub == j, dots, blk)
                act_ref[pl.ds(pl.multiple_of(i * PEER_UNROLL + jb, SUB), SUB), :] = blk
            return carry

        lax.fori_loop(0, tm // PEER_UNROLL, body, 0)

    _with_index_tile(e_hbm, (e_a, e_b), sems, tm, gather_loop)
    act = act_ref[...]
    gelu = 0.5 * act * (1.0 + lax.erf(act * (2.0 ** -0.5)))
    o_ref[...] = gate_ref[...] * gelu


def _peer_out_kernel(e_hbm, h_ref, gw_ref, tbl_ref, exp_ref, gfin_ref, o_ref, ge_ref, acc_ref, e_a, e_b, sems):
    tm = h_ref.shape[0]
    n_grp = e_a.shape[0] // PEER_GROUP
    ge_ref[...] = _split_dot(gw_ref[...], exp_ref[...])
    mask = _diag_mask((SUB, n_grp * LANES))

    def gather_loop(e_ref):
        def body(i, carry):
            for j in range(PEER_UNROLL):
                t = i * PEER_UNROLL + j
                lhs = jnp.where(mask, jnp.broadcast_to(ge_ref[pl.ds(t, 1), :], mask.shape), 0.0).astype(BF16)
                acc = jnp.zeros((SUB, LANES), F32)
                for g in range(n_grp):
                    w = _gather_rows(tbl_ref, e_ref, t, g)
                    acc = acc + jnp.dot(lhs[:, g * LANES:(g + 1) * LANES], w, preferred_element_type=F32)
                acc_ref[t] = acc
            return carry

        lax.fori_loop(0, tm // PEER_UNROLL, body, 0)

    _with_index_tile(e_hbm, (e_a, e_b), sems, tm, gather_loop)
    y = h_ref[...] + acc_ref[...]
    ms = jnp.sum(jnp.sum(y * y, axis=2, keepdims=True), axis=1, keepdims=True) / (y.shape[1] * y.shape[2])
    o_ref[...] = y * lax.rsqrt(ms + RMS_EPS) * gfin_ref[...]


def _table_tiles(t):
    e, d = t.shape
    p = d // (2 * LANES)
    tb = t.astype(BF16).reshape(e, p, 2, LANES).transpose(0, 1, 3, 2)
    return lax.bitcast_convert_type(tb, jnp.int32).reshape(e * p, LANES)


def _peer(h2, e_idx, gate, g_ffn, g_final, u_tiles, v_tiles, tm):
    n, d = h2.shape
    kk = e_idx.shape[0]
    h3 = h2.reshape(n, SUB, d // SUB)
    row3 = pl.BlockSpec((tm, SUB, d // SUB), lambda i: (i, 0, 0))
    row = pl.BlockSpec((tm, kk), lambda i: (i, 0))
    idx = pl.BlockSpec((kk, tm), lambda i: (0, i), memory_space=pltpu.SMEM)
    lane = jnp.arange(kk * SUB) // SUB
    sel = (lane[:, None] == jnp.arange(kk)[None, :]).astype(BF16)
    gw = pl.pallas_call(
        _peer_act_kernel,
        grid=(n // tm,),
        in_specs=[pl.BlockSpec(memory_space=pl.ANY), row3, _const_spec((1, SUB, d // SUB)), row,
                  _const_spec(u_tiles.shape)],
        out_specs=row,
        out_shape=jax.ShapeDtypeStruct((n, kk), F32),
        scratch_shapes=[pltpu.VMEM((tm, kk), F32), pltpu.VMEM((tm, 2 * SUB, d // SUB), BF16),
                        pltpu.SMEM((kk, tm), jnp.int32), pltpu.SMEM((kk, tm), jnp.int32),
                        pltpu.SemaphoreType.DMA((2,))],
        compiler_params=_cparams("arbitrary"),
        name="peer_act",
    )(e_idx, h3, g_ffn.reshape(1, SUB, d // SUB), gate, u_tiles)
    out = pl.pallas_call(
        _peer_out_kernel,
        grid=(n // tm,),
        in_specs=[pl.BlockSpec(memory_space=pl.ANY), row3, row, _const_spec(v_tiles.shape), _const_spec(sel.T.shape),
                  _const_spec((1, SUB, d // SUB))],
        out_specs=row3,
        out_shape=jax.ShapeDtypeStruct((n, SUB, d // SUB), F32),
        scratch_shapes=[pltpu.VMEM((tm, kk * SUB), F32), pltpu.VMEM((tm, SUB, d // SUB), F32),
                        pltpu.SMEM((kk, tm), jnp.int32), pltpu.SMEM((kk, tm), jnp.int32),
                        pltpu.SemaphoreType.DMA((2,))],
        compiler_params=_cparams("arbitrary"),
        name="peer_out",
    )(e_idx, h3, gw, v_tiles, sel.T, g_final.reshape(1, SUB, d // SUB))
    return out.reshape(n, d)


def kernel(x, mem, positions, g_mix, w_in, attn_sinks, rwkv_mu, rwkv_w0, rwkv_w2, rwkv_a0, rwkv_a2, rwkv_g2, rwkv_k_k, rwkv_k_a, rwkv_r_k, rwkv_ln_w, rwkv_ln_b, w_out, g_cross, g_mem, w_q_cross, w_kv_cross, w_o_cross, g_ffn, peer_w_q, peer_sub_keys_1, peer_sub_keys_2, peer_u, peer_v, g_final):
    b, s, d = x.shape
    n = b * s
    L = 0
    wq = d
    wk = ATT_KV_HEADS * HEAD_DIM
    wp = 3 * d + DECAY_LORA + AAA_LORA + GATE_LORA
    widths = (wq, wk, wk, wp, 2 * d)
    rc, rs = _rope_tables(positions)
    q, k, v, p, gates = _in_proj(x.reshape(n, d), g_mix[L], w_in[L].astype(BF16), rc, rs, widths, 256)
    attn = _swa(q, k, v, attn_sinks[L], b, s)
    rw = _rwkv(p, rwkv_mu[L], rwkv_w0[L], rwkv_w2[L], rwkv_a0[L], rwkv_a2[L], rwkv_g2[L], rwkv_k_k[L], rwkv_k_a[L],
               rwkv_r_k[L], rwkv_ln_w[L], rwkv_ln_b[L], b, s)
    m = mem.shape[1]
    kv = _mem_kv(mem.reshape(b * m, d), g_mem[L], w_kv_cross[L].astype(BF16), m)
    h2, s1, s2 = _post(x.reshape(n, d), attn, rw, gates, kv, w_out[L].astype(BF16), g_cross[L],
                       w_q_cross[L].astype(BF16), w_o_cross[L].astype(BF16), g_ffn[L], peer_w_q[L].astype(BF16),
                       peer_sub_keys_1[L], peer_sub_keys_2[L], b, s, m, 256)
    e_t, g_t = _route(s1, s2, LANES)
    kk = PEER_HEADS * PEER_TOPK
    e_idx = e_t.reshape(kk, n) * 4
    gate = g_t.reshape(kk, n).T
    out = _peer(h2, e_idx, gate, g_ffn[L], g_final, _table_tiles(peer_u[L]), _table_tiles(peer_v[L]), LANES)
    return out.reshape(b, s, d)
```

```python
import functools
import math

import jax
import jax.numpy as jnp
from jax import lax
from jax.experimental import pallas as pl
from jax.experimental.pallas import tpu as pltpu

F32 = jnp.float32
BF16 = jnp.bfloat16

RMS_EPS = 1e-5
HEAD_DIM = 64
ATT_KV_HEADS = 4
WINDOW = 128
ROT_DIM = HEAD_DIM // 4
ROPE_THETA = 500000.0
RWKV_HEAD_DIM = 64
DECAY_LORA = 64
AAA_LORA = 64
GATE_LORA = 128
RWKV_GN_EPS = 64e-5
CROSS_HEADS = 4
PEER_HEADS = 8
PEER_TOPK = 16

LANES = 128
SUBLANES = 8
VMEM_LIMIT = 56 * 1024 * 1024


def _cparams(*sem):
    return pltpu.CompilerParams(dimension_semantics=sem, vmem_limit_bytes=VMEM_LIMIT)


def _const_spec(shape):
    nd = len(shape)
    return pl.BlockSpec(shape, lambda *_: (0,) * nd, pipeline_mode=pl.Buffered(1))


def _rms(x, g):
    return x * lax.rsqrt(jnp.mean(x * x, axis=-1, keepdims=True) + RMS_EPS) * g


def _sigmoid(x):
    return 1.0 / (1.0 + jnp.exp(-x))


def _rope(t, c, s):
    w = t.shape[-1]
    rep = w // LANES
    cc = jnp.tile(c, (1, rep))
    ss = jnp.tile(s, (1, rep))
    lane = lax.broadcasted_iota(jnp.int32, t.shape, 1)
    first = (lane % HEAD_DIM) < (ROT_DIM // 2)
    partner = jnp.where(first, pltpu.roll(t, w - ROT_DIM // 2, axis=1), pltpu.roll(t, ROT_DIM // 2, axis=1))
    return t * cc + partner * ss


def _inproj_kernel(x_ref, g_ref, w_ref, c_ref, s_ref, q_ref, k_ref, v_ref, p_ref, gate_ref, *, widths):
    wq, wk, wv, wp, wg = widths
    xb = _rms(x_ref[...], g_ref[...]).astype(BF16)
    c = c_ref[...]
    s = s_ref[...]
    o = 0
    q = jnp.dot(xb, w_ref[:, o:o + wq], preferred_element_type=F32)
    q_ref[...] = _rope(q, c, s).astype(q_ref.dtype)
    o += wq
    k = jnp.dot(xb, w_ref[:, o:o + wk], preferred_element_type=F32)
    k_ref[...] = _rope(k, c, s).astype(k_ref.dtype)
    o += wk
    v_ref[...] = jnp.dot(xb, w_ref[:, o:o + wv], preferred_element_type=F32).astype(v_ref.dtype)
    o += wv
    p_ref[...] = jnp.dot(xb, w_ref[:, o:o + wp], preferred_element_type=F32)
    o += wp
    gate_ref[...] = _sigmoid(jnp.dot(xb, w_ref[:, o:o + wg], preferred_element_type=F32)).astype(gate_ref.dtype)


def _in_proj(x2, g_mix, w_in_b, rope_c, rope_s, widths, tm):
    n, d = x2.shape
    wq, wk, wv, wp, wg = widths
    row = lambda w: pl.BlockSpec((tm, w), lambda i: (i, 0))
    return pl.pallas_call(
        functools.partial(_inproj_kernel, widths=widths),
        grid=(n // tm,),
        in_specs=[row(d), _const_spec((1, d)), _const_spec(w_in_b.shape), row(LANES), row(LANES)],
        out_specs=[row(wq), row(wk), row(wv), row(wp), row(wg)],
        out_shape=[jax.ShapeDtypeStruct((n, wq), BF16), jax.ShapeDtypeStruct((n, wk), BF16),
                   jax.ShapeDtypeStruct((n, wv), BF16), jax.ShapeDtypeStruct((n, wp), F32),
                   jax.ShapeDtypeStruct((n, wg), BF16)],
        compiler_params=_cparams("parallel"),
        name="in_proj",
    )(x2, g_mix.reshape(1, d), w_in_b, rope_c, rope_s)


def _rope_tables(positions):
    half = ROT_DIM // 2
    inv_freq = 1.0 / (ROPE_THETA ** (jnp.arange(0, ROT_DIM, 2, dtype=F32) / ROT_DIM))
    ang = positions.reshape(-1).astype(F32)[:, None] * inv_freq
    cos, sin = jnp.cos(ang), jnp.sin(ang)
    n = ang.shape[0]
    pad1 = jnp.ones((n, HEAD_DIM - ROT_DIM), F32)
    pad0 = jnp.zeros((n, HEAD_DIM - ROT_DIM), F32)
    c = jnp.concatenate([cos, cos, pad1], axis=1)
    s = jnp.concatenate([-sin, sin, pad0], axis=1)
    return jnp.tile(c, (1, LANES // HEAD_DIM)), jnp.tile(s, (1, LANES // HEAD_DIM))


def _swa_kernel(sink_ref, q_ref, kp_ref, kc_ref, vp_ref, vc_ref, o_ref, *, group):
    n = pl.program_id(1)
    blk = q_ref.shape[0]
    scale = HEAD_DIM ** -0.5
    rows = group * blk
    qi = lax.broadcasted_iota(jnp.int32, (rows, 2 * blk), 0) % blk
    kj = lax.broadcasted_iota(jnp.int32, (rows, 2 * blk), 1)
    valid = (kj > qi) & (kj <= qi + WINDOW) & ((kj >= blk) | (n > 0))
    rid = lax.broadcasted_iota(jnp.int32, (rows, 1), 0) // blk
    KV = range(ATT_KV_HEADS)
    sls = [slice(kv * HEAD_DIM, (kv + 1) * HEAD_DIM) for kv in KV]
    heads = [[kv * group + g for g in range(group)] for kv in KV]
    s = []
    for kv in KV:
        kk = jnp.concatenate([kp_ref[:, sls[kv]], kc_ref[:, sls[kv]]], axis=0)
        qq = jnp.concatenate([q_ref[:, h * HEAD_DIM:(h + 1) * HEAD_DIM] for h in heads[kv]], axis=0)
        s.append(lax.dot_general(qq, kk, (((1,), (1,)), ((), ())), preferred_element_type=F32))
    p, inv = [], []
    for kv in KV:
        sc = jnp.where(valid, s[kv] * scale, -1e30)
        sink = jnp.zeros((rows, 1), F32)
        for g, h in enumerate(heads[kv]):
            sink = jnp.where(rid == g, sink_ref[h], sink)
        m = jnp.maximum(jnp.max(sc, axis=-1, keepdims=True), sink)
        e = jnp.exp(sc - m)
        inv.append(1.0 / (jnp.sum(e, axis=-1, keepdims=True) + jnp.exp(sink - m)))
        p.append(e.astype(BF16))
    for kv in KV:
        vv = jnp.concatenate([vp_ref[:, sls[kv]], vc_ref[:, sls[kv]]], axis=0)
        o = jnp.dot(p[kv], vv, preferred_element_type=F32) * inv[kv]
        for g, h in enumerate(heads[kv]):
            o_ref[:, h * HEAD_DIM:(h + 1) * HEAD_DIM] = o[g * blk:(g + 1) * blk].astype(o_ref.dtype)


def _swa(q, k, v, sinks, batch, seq):
    n, wq = q.shape
    wk = k.shape[1]
    blk = WINDOW
    nb = seq // blk
    group = (wq // HEAD_DIM) // ATT_KV_HEADS
    cur = lambda w: pl.BlockSpec((blk, w), lambda b, i: (b * nb + i, 0))
    prev = lambda w: pl.BlockSpec((blk, w), lambda b, i: (b * nb + jnp.maximum(i - 1, 0), 0))
    return pl.pallas_call(
        functools.partial(_swa_kernel, group=group),
        grid=(batch, nb),
        in_specs=[pl.BlockSpec(memory_space=pltpu.SMEM), cur(wq), prev(wk), cur(wk), prev(wk), cur(wk)],
        out_specs=cur(wq),
        out_shape=jax.ShapeDtypeStruct((n, wq), BF16),
        compiler_params=_cparams("parallel", "parallel"),
        name="swa",
    )(sinks.astype(F32), q, k, k, v, v)


RW_CHUNK = 64
RW_SEQS = 2


def _mm(a, b):
    return jnp.dot(a.astype(BF16), b.astype(BF16), preferred_element_type=F32)


def _mm_nt(a, b):
    return lax.dot_general(a.astype(BF16), b.astype(BF16), (((1,), (1,)), ((), ())), preferred_element_type=F32)


def _mm_tn(a, b):
    return lax.dot_general(a.astype(BF16), b.astype(BF16), (((0,), (0,)), ((), ())), preferred_element_type=F32)


def _hi_lo(a):
    hi = a.astype(BF16)
    return hi, (a - hi.astype(F32)).astype(BF16)


def _split_dot(a, b01):
    hi, lo = _hi_lo(a)
    return jnp.dot(hi, b01, preferred_element_type=F32) + jnp.dot(lo, b01, preferred_element_type=F32)


def _split_dot_left(a01, b):
    hi, lo = _hi_lo(b)
    return jnp.dot(a01, hi, preferred_element_type=F32) + jnp.dot(a01, lo, preferred_element_type=F32)


def _rwkv_kernel(p_ref, mu_ref, w0_ref, w2_ref, a0_ref, a2_ref, g2_ref, kk_ref, ka_ref, rk_ref, lnw_ref, lnb_ref,
                 o_ref, last_ref, st_ref, *, width):
    c = pl.program_id(1)
    NB, C, wp = p_ref.shape
    R = NB * C
    W = width
    N = RWKV_HEAD_DIM
    SL = 2 * N
    n_slab = W // SL

    @pl.when(c == 0)
    def _():
        last_ref[...] = jnp.zeros_like(last_ref)
        st_ref[...] = jnp.zeros_like(st_ref)

    p = p_ref[...].reshape(R, wp)
    row = lax.broadcasted_iota(jnp.int32, (R, 1), 0)
    carry = jnp.concatenate([jnp.broadcast_to(last_ref[b:b + 1, :], (C, wp)) for b in range(NB)], axis=0)
    prev = jnp.where(row % C == 0, carry, pltpu.roll(p, 1, axis=0))
    for b in range(NB):
        last_ref[b:b + 1, :] = p[(b + 1) * C - 1:(b + 1) * C, :]
    p = p + (prev - p) * mu_ref[...]
    r = p[:, :W]
    k = p[:, W:2 * W]
    v = p[:, 2 * W:3 * W]
    o3 = 3 * W
    dw = p[:, o3:o3 + DECAY_LORA]
    da = p[:, o3 + DECAY_LORA:o3 + DECAY_LORA + AAA_LORA]
    dg = p[:, o3 + DECAY_LORA + AAA_LORA:]

    z = -(w0_ref[...] + _mm(jnp.tanh(dw), w2_ref[...]))
    w_log = -(jnp.maximum(z, 0.0) + jnp.log(1.0 + jnp.exp(-jnp.abs(z)))) - 0.5
    lw = -jnp.exp(w_log)
    a = _sigmoid(a0_ref[...] + _mm(da, a2_ref[...]))
    g = _mm(_sigmoid(dg), g2_ref[...])

    ti = lax.broadcasted_iota(jnp.int32, (R, R), 0)
    tj = lax.broadcasted_iota(jnp.int32, (R, R), 1)
    tri = ((ti >= tj) & (ti // C == tj // C)).astype(BF16)
    cum = _split_dot_left(tri, lw)
    e_pos = jnp.exp(cum)
    e_neg = jnp.exp(-cum)
    e_prev = jnp.exp(cum - lw)

    li = lax.broadcasted_iota(jnp.int32, (SL, SL), 0)
    lj = lax.broadcasted_iota(jnp.int32, (SL, SL), 1)
    same = (li // N) == (lj // N)
    head_sum = same.astype(BF16)
    strict = same & ((li % N) > (lj % N))
    incl = same & ((li % N) >= (lj % N))
    lane = lax.broadcasted_iota(jnp.int32, (C, SL), 1)
    m0 = lane < N

    def stack(t):
        return jnp.concatenate([jnp.where(m0, t, 0.0), jnp.where(m0, 0.0, t)], axis=0)

    items = [(b, s) for b in range(NB) for s in range(n_slab)]
    S = range(len(items))
    at = [(slice(b * C, (b + 1) * C), slice(s * SL, (s + 1) * SL)) for b, s in items]
    lanes = [sl for _, sl in at]
    h2 = 2 * C
    bf = lambda t: t.astype(BF16)
    e_end = [jnp.exp(cum[rw.stop - 1:rw.stop, sl] - cum[rw, sl]) for rw, sl in at]
    x = [k[i] * kk_ref[:, i[1]] for i in at]
    k2 = [k[i] * (1.0 + (a[i] - 1.0) * ka_ref[:, i[1]]) for i in at]
    sums = [_split_dot(jnp.concatenate([x[s] * x[s], r[at[s]] * k2[s] * rk_ref[:, lanes[s]]], axis=0), head_sum)
            for s in S]
    kk = [x[s] / jnp.maximum(jnp.sqrt(sums[s][:C]), 1e-12) for s in S]
    kb = [kk[s] * a[at[s]] for s in S]
    AR = [bf(jnp.concatenate([stack(-kk[s] * e_prev[at[s]]), stack(r[at[s]] * e_pos[at[s]])], axis=0))
          for s in S]
    twice = lambda t: jnp.concatenate([t, t], axis=0)
    BK = [bf(jnp.concatenate([twice(kb[s] * e_neg[at[s]]), twice(k2[s] * e_neg[at[s]])], axis=0)) for s in S]
    V2 = [bf(stack(v[i])) for i in at]
    Be = [bf(twice(kb[s] * e_end[s])) for s in S]
    Ke = [bf(twice(k2[s] * e_end[s])) for s in S]
    st = [st_ref[s] for s in S]
    G = [_mm_nt(AR[s], BK[s]) for s in S]
    ST = [_mm(AR[s], st[s]) for s in S]
    A_k = [bf(jnp.concatenate([jnp.where(strict, G[s][:h2, h2:], 0.0), jnp.where(incl, G[s][h2:, h2:], 0.0)],
                              axis=0)) for s in S]
    AV = [_mm(A_k[s], V2[s]) for s in S]
    U = [ST[s][:h2] + AV[s][:h2] for s in S]
    Ap = [bf(jnp.where(strict, G[s][:h2, :h2], 0.0)) for s in S]
    steps = int(math.log2(C))
    for i in range(steps - 1):
        Rm = [_mm(Ap[s], jnp.concatenate([Ap[s], bf(U[s])], axis=1)) for s in S]
        Ap = [bf(Rm[s][:, :SL]) for s in S]
        U = [U[s] + Rm[s][:, SL:] for s in S]
    U = [U[s] + _mm(Ap[s], U[s]) for s in S]
    Ub = [bf(U[s]) for s in S]
    A_rb = [bf(jnp.where(incl, G[s][h2:, :h2], 0.0)) for s in S]
    O2 = [ST[s][h2:] + AV[s][h2:] + _mm(A_rb[s], Ub[s]) for s in S]
    for s in S:
        rw, sl = at[s]
        decay = jnp.transpose(jnp.broadcast_to(e_pos[rw.stop - 1:rw.stop, sl], (SL, SL)))
        st_new = st[s] * decay + _mm_tn(Be[s], Ub[s]) + _mm_tn(Ke[s], V2[s])
        st_ref[s] = jnp.where(same, st_new, 0.0)
    o = [O2[s][:C] + O2[s][C:] for s in S]
    mean = [_split_dot(o[s], head_sum) * (1.0 / N) for s in S]
    dlt = [o[s] - mean[s] for s in S]
    var = [_split_dot(dlt[s] * dlt[s], head_sum) * (1.0 / N) for s in S]
    bonus = [sums[s][C:] * v[at[s]] for s in S]
    for s in S:
        (b, _), sl = items[s], lanes[s]
        on = dlt[s] * lax.rsqrt(var[s] + RWKV_GN_EPS) * lnw_ref[:, sl] + lnb_ref[:, sl]
        o_ref[b, :, sl] = ((on + bonus[s]) * g[at[s]]).astype(o_ref.dtype)


def _rwkv(p, mu, w0, w2, a0, a2, g2, k_k, k_a, r_k, ln_w, ln_b, batch, seq):
    n, wp = p.shape
    width = w0.shape[-1]
    C = RW_CHUNK
    nb = RW_SEQS
    vec = lambda t: t.reshape(1, -1).astype(F32)
    args = [vec(mu), vec(w0), w2.astype(BF16), vec(a0), a2.astype(BF16), g2.astype(BF16), vec(k_k), vec(k_a),
            vec(r_k), vec(ln_w), vec(ln_b)]
    out = pl.pallas_call(
        functools.partial(_rwkv_kernel, width=width),
        grid=(batch // nb, seq // C),
        in_specs=[pl.BlockSpec((nb, C, wp), lambda b, c: (b, c, 0))] + [_const_spec(t.shape) for t in args],
        out_specs=pl.BlockSpec((nb, C, width), lambda b, c: (b, c, 0)),
        out_shape=jax.ShapeDtypeStruct((batch, seq, width), BF16),
        scratch_shapes=[pltpu.VMEM((nb, wp), F32),
                        pltpu.VMEM((nb * width // (2 * RWKV_HEAD_DIM), 2 * RWKV_HEAD_DIM, 2 * RWKV_HEAD_DIM), F32)],
        compiler_params=_cparams("parallel", "arbitrary"),
        name="rwkv",
    )(p.reshape(batch, seq, wp), *args)
    return out.reshape(n, width)


def _memkv_kernel(m_ref, g_ref, w_ref, o_ref):
    o_ref[...] = jnp.dot(_rms(m_ref[...], g_ref[...]).astype(BF16), w_ref[...],
                         preferred_element_type=F32).astype(o_ref.dtype)


def _mem_kv(mem2, g_mem, w_kv_b, mem_len):
    n, d = mem2.shape
    wo = w_kv_b.shape[1]
    return pl.pallas_call(
        _memkv_kernel,
        grid=(n // mem_len,),
        in_specs=[pl.BlockSpec((mem_len, d), lambda i: (i, 0)), _const_spec((1, d)), _const_spec(w_kv_b.shape)],
        out_specs=pl.BlockSpec((mem_len, wo), lambda i: (i, 0)),
        out_shape=jax.ShapeDtypeStruct((n, wo), BF16),
        compiler_params=_cparams("parallel"),
        name="mem_kv",
    )(mem2, g_mem.reshape(1, d), w_kv_b)


def _post_kernel(x_ref, at_ref, rw_ref, gt_ref, kv_ref, wo_ref, gc_ref, wq_ref, woc_ref, gf_ref, wpq_ref,
                 k1_ref, k2_ref, h_ref, s1_ref, s2_ref):
    d = x_ref.shape[1]
    ga = gt_ref[:, :d].astype(F32)
    gb = gt_ref[:, d:].astype(F32)
    mixed = ga * at_ref[...].astype(F32) + gb * rw_ref[...].astype(F32)
    h1 = x_ref[...] + jnp.dot(mixed.astype(BF16), wo_ref[...], preferred_element_type=F32)

    qc = jnp.dot(_rms(h1, gc_ref[...]).astype(BF16), wq_ref[...], preferred_element_type=F32)
    wc = qc.shape[1]
    hd = wc // CROSS_HEADS
    scale = hd ** -0.5
    H = range(CROSS_HEADS)
    qb = (qc * scale).astype(BF16)
    s = [lax.dot_general(qb[:, hh * hd:(hh + 1) * hd], kv_ref[:, hh * hd:(hh + 1) * hd],
                         (((1,), (1,)), ((), ())), preferred_element_type=F32) for hh in H]
    ex = [jnp.exp(s[hh] - jnp.max(s[hh], axis=-1, keepdims=True)) for hh in H]
    inv = [1.0 / jnp.sum(ex[hh], axis=-1, keepdims=True) for hh in H]
    outs = [jnp.dot(ex[hh].astype(BF16), kv_ref[:, wc + hh * hd:wc + (hh + 1) * hd],
                    preferred_element_type=F32) * inv[hh] for hh in H]
    oc = jnp.concatenate(outs, axis=1)
    h2 = h1 + jnp.dot(oc.astype(BF16), woc_ref[...], preferred_element_type=F32)
    h_ref[...] = h2

    q3 = jnp.dot(_rms(h2, gf_ref[...]).astype(BF16), wpq_ref[...], preferred_element_type=F32)
    half = k1_ref.shape[1]
    dn = (((1,), (1,)), ((), ()))
    q_hi, q_lo = _hi_lo(q3)
    k_parts = [_hi_lo(k1_ref[...]), _hi_lo(k2_ref[...])]

    def score(kp, c0):
        qh, ql = q_hi[:, c0:c0 + half], q_lo[:, c0:c0 + half]
        return (lax.dot_general(kp[0], qh, dn, preferred_element_type=F32)
                + lax.dot_general(kp[0], ql, dn, preferred_element_type=F32)
                + lax.dot_general(kp[1], qh, dn, preferred_element_type=F32))

    for hh in range(PEER_HEADS):
        s1_ref[hh] = score(k_parts[0], 2 * hh * half)
        s2_ref[hh] = score(k_parts[1], (2 * hh + 1) * half)


def _post(x2, attn, rw, gates, kv, w_out_b, g_cross, w_qc_b, w_oc_b, g_ffn, w_pq_b, k1, k2, batch, seq, mem_len, tm):
    n, d = x2.shape
    nt = seq // tm
    n_keys = k1.shape[0]
    row = lambda w: pl.BlockSpec((tm, w), lambda b, i: (b * nt + i, 0))
    sc_spec = pl.BlockSpec((PEER_HEADS, n_keys, tm), lambda b, i: (0, 0, b * nt + i))
    consts = [w_out_b, g_cross.reshape(1, d), w_qc_b, w_oc_b, g_ffn.reshape(1, d), w_pq_b, k1.astype(F32), k2.astype(F32)]
    return pl.pallas_call(
        _post_kernel,
        grid=(batch, nt),
        in_specs=[row(d), row(d), row(d), row(2 * d), pl.BlockSpec((mem_len, kv.shape[1]), lambda b, i: (b, 0))]
                 + [_const_spec(t.shape) for t in consts],
        out_specs=[row(d), sc_spec, sc_spec],
        out_shape=[jax.ShapeDtypeStruct((n, d), F32), jax.ShapeDtypeStruct((PEER_HEADS, n_keys, n), F32),
                   jax.ShapeDtypeStruct((PEER_HEADS, n_keys, n), F32)],
        compiler_params=_cparams("parallel", "parallel"),
        name="post",
    )(x2, attn, rw, gates, kv, *consts)


ROUTE_UNROLL = 4
TABLE_ROWS_PER_EXPERT = 4


def _topk_rows(s, k, payload=None):
    rows = s.shape[0]
    rid = lax.broadcasted_iota(jnp.int32, s.shape, 0).astype(F32)
    vals, outs = [], []
    for _ in range(k):
        m = jnp.max(s, axis=0, keepdims=True)
        i = jnp.min(jnp.where(s == m, rid, float(rows)), axis=0, keepdims=True)
        hit = rid == i
        vals.append(m)
        outs.append(i if payload is None else jnp.max(jnp.where(hit, payload, -1.0), axis=0, keepdims=True))
        s = jnp.where(hit, -jnp.inf, s)
    return jnp.concatenate(vals, axis=0), jnp.concatenate(outs, axis=0)


def _route_kernel(s1_ref, s2_ref, e_ref, g_ref):
    K = PEER_TOPK
    n_keys = s1_ref.shape[1]

    def one_head(h):
        v1, i1 = _topk_rows(s1_ref[h], K)
        v2, i2 = _topk_rows(s2_ref[h], K)
        cand, ecand = [], []
        for a in range(K):
            nb = K // (a + 1)
            cand.append(v1[a:a + 1, :] + v2[:nb, :])
            ecand.append(i1[a:a + 1, :] * float(n_keys) + i2[:nb, :])
        n_c = sum(c.shape[0] for c in cand)
        pad = -n_c % SUBLANES
        if pad:
            cand.append(jnp.full((pad, v1.shape[1]), -jnp.inf, F32))
            ecand.append(jnp.full((pad, v1.shape[1]), -1.0, F32))
        sc, e = _topk_rows(jnp.concatenate(cand, axis=0), K, jnp.concatenate(ecand, axis=0))
        e_ref[h] = (e * float(TABLE_ROWS_PER_EXPERT)).astype(jnp.int32)
        ex = jnp.exp(sc - sc[0:1, :])
        g_ref[h] = ex / jnp.sum(ex, axis=0, keepdims=True)

    def body(i, carry):
        for j in range(ROUTE_UNROLL):
            one_head(i * ROUTE_UNROLL + j)
        return carry

    lax.fori_loop(0, s1_ref.shape[0] // ROUTE_UNROLL, body, 0)


def _route(s1, s2, tk):
    heads, n_keys, n = s1.shape
    spec_in = pl.BlockSpec((heads, n_keys, tk), lambda i: (0, 0, i))
    spec_out = pl.BlockSpec((heads, PEER_TOPK, tk), lambda i: (0, 0, i))
    return pl.pallas_call(
        _route_kernel,
        grid=(n // tk,),
        in_specs=[spec_in, spec_in],
        out_specs=[spec_out, spec_out],
        out_shape=[jax.ShapeDtypeStruct((heads, PEER_TOPK, n), jnp.int32),
                   jax.ShapeDtypeStruct((heads, PEER_TOPK, n), F32)],
        compiler_params=_cparams("parallel"),
        name="route",
    )(s1, s2)


PEER_GROUP = 16
SUB = SUBLANES
PEER_UNROLL = 32


def _gather_rows(tbl_ref, e_ref, t, g):
    rows = TABLE_ROWS_PER_EXPERT
    parts = [tbl_ref[pl.ds(pl.multiple_of(e_ref.at[g * PEER_GROUP + j][t], rows), rows), :]
             for j in range(PEER_GROUP)]
    return pltpu.bitcast(jnp.concatenate(parts, axis=0), BF16)


def _diag_mask(shape):
    r = lax.broadcasted_iota(jnp.int32, shape, len(shape) - 2)
    c = lax.broadcasted_iota(jnp.int32, shape, len(shape) - 1)
    return r == (c % SUB)


def _with_index_tile(e_hbm, bufs, sems, tm, fn):
    i = pl.program_id(0)

    def copy(step, slot):
        return pltpu.make_async_copy(e_hbm.at[:, pl.ds(pl.multiple_of(step * tm, tm), tm)], bufs[slot], sems.at[slot])

    @pl.when(i == 0)
    def _():
        copy(0, 0).start()

    for slot in range(2):
        @pl.when(i % 2 == slot)
        def _(slot=slot):
            @pl.when(i + 1 < pl.num_programs(0))
            def _():
                copy(i + 1, 1 - slot).start()

            copy(i, slot).wait()
            fn(bufs[slot])


def _peer_act_kernel(e_hbm, h_ref, gf_ref, gate_ref, tbl_ref, o_ref, act_ref, x2_ref, e_a, e_b, sems):
    tm = h_ref.shape[0]
    n_grp = e_a.shape[0] // PEER_GROUP
    h = h_ref[...]
    ms = jnp.sum(jnp.sum(h * h, axis=2, keepdims=True), axis=1, keepdims=True) / (h.shape[1] * h.shape[2])
    xn = (h * lax.rsqrt(ms + RMS_EPS) * gf_ref[...]).astype(BF16)
    x2_ref[...] = jnp.concatenate([xn, xn], axis=1)
    kr = lax.broadcasted_iota(jnp.int32, (PEER_GROUP, PEER_GROUP * SUB), 0)
    kc = lax.broadcasted_iota(jnp.int32, (PEER_GROUP, PEER_GROUP * SUB), 1) // SUB
    row_sum = (kr == kc).astype(BF16)
    sub = lax.broadcasted_iota(jnp.int32, (SUB, LANES), 0)

    def gather_loop(e_ref):
        def body(i, carry):
            qs = []
            for j in range(PEER_UNROLL):
                t = i * PEER_UNROLL + j
                xt = jnp.tile(x2_ref[t], (PEER_GROUP // 2, 1))
                qs.append(jnp.concatenate([jnp.dot(row_sum, _gather_rows(tbl_ref, e_ref, t, g) * xt,
                                                   preferred_element_type=F32) for g in range(n_grp)], axis=0))
            for jb in range(0, PEER_UNROLL, SUB):
                blk = jnp.zeros((SUB, LANES), F32)
                for j in range(SUB):
                    dots = jnp.sum(qs[jb + j].T, axis=0, keepdims=True)
                    blk = jnp.where(sub == j, dots, blk)
                act_ref[pl.ds(pl.multiple_of(i * PEER_UNROLL + jb, SUB), SUB), :] = blk
            return carry

        lax.fori_loop(0, tm // PEER_UNROLL, body, 0)

    _with_index_tile(e_hbm, (e_a, e_b), sems, tm, gather_loop)
    act = act_ref[...]
    gelu = 0.5 * act * (1.0 + lax.erf(act * (2.0 ** -0.5)))
    o_ref[...] = gate_ref[...].T * gelu


def _peer_out_kernel(e_hbm, h_ref, gw_ref, tbl_ref, exp_ref, gfin_ref, o_ref, ge_ref, acc_ref, e_a, e_b, sems):
    tm = h_ref.shape[0]
    n_grp = e_a.shape[0] // PEER_GROUP
    ge_ref[...] = _split_dot(gw_ref[...], exp_ref[...])
    mask = _diag_mask((SUB, n_grp * LANES))

    def gather_loop(e_ref):
        def body(i, carry):
            for j in range(PEER_UNROLL):
                t = i * PEER_UNROLL + j
                lhs = jnp.where(mask, jnp.broadcast_to(ge_ref[pl.ds(t, 1), :], mask.shape), 0.0).astype(BF16)
                acc = jnp.zeros((SUB, LANES), F32)
                for g in range(n_grp):
                    w = _gather_rows(tbl_ref, e_ref, t, g)
                    acc = acc + jnp.dot(lhs[:, g * LANES:(g + 1) * LANES], w, preferred_element_type=F32)
                acc_ref[t] = acc
            return carry

        lax.fori_loop(0, tm // PEER_UNROLL, body, 0)

    _with_index_tile(e_hbm, (e_a, e_b), sems, tm, gather_loop)
    y = h_ref[...] + acc_ref[...]
    ms = jnp.sum(jnp.sum(y * y, axis=2, keepdims=True), axis=1, keepdims=True) / (y.shape[1] * y.shape[2])
    o_ref[...] = y * lax.rsqrt(ms + RMS_EPS) * gfin_ref[...]


def _table_tiles(t):
    e, d = t.shape
    p = d // (2 * LANES)
    tb = t.astype(BF16).reshape(e, p, 2, LANES)
    lo = lax.bitcast_convert_type(tb[:, :, 0, :], jnp.uint16).astype(jnp.uint32)
    hi = lax.bitcast_convert_type(tb[:, :, 1, :], jnp.uint16).astype(jnp.uint32)
    return lax.bitcast_convert_type(lo | (hi << 16), jnp.int32).reshape(e * p, LANES)


def _peer(h2, e_idx, gate, g_ffn, g_final, u_tiles, v_tiles, tm):
    n, d = h2.shape
    kk = e_idx.shape[0]
    h3 = h2.reshape(n, SUB, d // SUB)
    row3 = pl.BlockSpec((tm, SUB, d // SUB), lambda i: (i, 0, 0))
    row = pl.BlockSpec((tm, kk), lambda i: (i, 0))
    idx = pl.BlockSpec((kk, tm), lambda i: (0, i), memory_space=pltpu.SMEM)
    lane = jnp.arange(kk * SUB) // SUB
    sel = (lane[:, None] == jnp.arange(kk)[None, :]).astype(BF16)
    gw = pl.pallas_call(
        _peer_act_kernel,
        grid=(n // tm,),
        in_specs=[pl.BlockSpec(memory_space=pl.ANY), row3, _const_spec((1, SUB, d // SUB)),
                  pl.BlockSpec((kk, tm), lambda i: (0, i)), _const_spec(u_tiles.shape)],
        out_specs=row,
        out_shape=jax.ShapeDtypeStruct((n, kk), F32),
        scratch_shapes=[pltpu.VMEM((tm, kk), F32), pltpu.VMEM((tm, 2 * SUB, d // SUB), BF16),
                        pltpu.SMEM((kk, tm), jnp.int32), pltpu.SMEM((kk, tm), jnp.int32),
                        pltpu.SemaphoreType.DMA((2,))],
        compiler_params=_cparams("arbitrary"),
        name="peer_act",
    )(e_idx, h3, g_ffn.reshape(1, SUB, d // SUB), gate, u_tiles)
    out = pl.pallas_call(
        _peer_out_kernel,
        grid=(n // tm,),
        in_specs=[pl.BlockSpec(memory_space=pl.ANY), row3, row, _const_spec(v_tiles.shape), _const_spec(sel.T.shape),
                  _const_spec((1, SUB, d // SUB))],
        out_specs=row3,
        out_shape=jax.ShapeDtypeStruct((n, SUB, d // SUB), F32),
        scratch_shapes=[pltpu.VMEM((tm, kk * SUB), F32), pltpu.VMEM((tm, SUB, d // SUB), F32),
                        pltpu.SMEM((kk, tm), jnp.int32), pltpu.SMEM((kk, tm), jnp.int32),
                        pltpu.SemaphoreType.DMA((2,))],
        compiler_params=_cparams("arbitrary"),
        name="peer_out",
    )(e_idx, h3, gw, v_tiles, sel.T, g_final.reshape(1, SUB, d // SUB))
    return out.reshape(n, d)


def kernel(x, mem, positions, g_mix, w_in, attn_sinks, rwkv_mu, rwkv_w0, rwkv_w2, rwkv_a0, rwkv_a2, rwkv_g2, rwkv_k_k, rwkv_k_a, rwkv_r_k, rwkv_ln_w, rwkv_ln_b, w_out, g_cross, g_mem, w_q_cross, w_kv_cross, w_o_cross, g_ffn, peer_w_q, peer_sub_keys_1, peer_sub_keys_2, peer_u, peer_v, g_final):
    b, s, d = x.shape
    n = b * s
    L = 0
    wq = d
    wk = ATT_KV_HEADS * HEAD_DIM
    wp = 3 * d + DECAY_LORA + AAA_LORA + GATE_LORA
    widths = (wq, wk, wk, wp, 2 * d)
    rc, rs = _rope_tables(positions)
    q, k, v, p, gates = _in_proj(x.reshape(n, d), g_mix[L], w_in[L].astype(BF16), rc, rs, widths, 256)
    attn = _swa(q, k, v, attn_sinks[L], b, s)
    rw = _rwkv(p, rwkv_mu[L], rwkv_w0[L], rwkv_w2[L], rwkv_a0[L], rwkv_a2[L], rwkv_g2[L], rwkv_k_k[L], rwkv_k_a[L],
               rwkv_r_k[L], rwkv_ln_w[L], rwkv_ln_b[L], b, s)
    m = mem.shape[1]
    kv = _mem_kv(mem.reshape(b * m, d), g_mem[L], w_kv_cross[L].astype(BF16), m)
    h2, s1, s2 = _post(x.reshape(n, d), attn, rw, gates, kv, w_out[L].astype(BF16), g_cross[L],
                       w_q_cross[L].astype(BF16), w_o_cross[L].astype(BF16), g_ffn[L], peer_w_q[L].astype(BF16),
                       peer_sub_keys_1[L], peer_sub_keys_2[L], b, s, m, 256)
    e_t, g_t = _route(s1, s2, LANES)
    kk = PEER_HEADS * PEER_TOPK
    assert d == TABLE_ROWS_PER_EXPERT * 2 * LANES
    out = _peer(h2, e_t.reshape(kk, n), g_t.reshape(kk, n), g_ffn[L], g_final, _table_tiles(peer_u[L]),
                _table_tiles(peer_v[L]), LANES)
    return out.reshape(b, s, d)
```

```python
import functools
import math

import jax
import jax.numpy as jnp
from jax import lax
from jax.experimental import pallas as pl
from jax.experimental.pallas import tpu as pltpu

F32 = jnp.float32
BF16 = jnp.bfloat16

RMS_EPS = 1e-5
HEAD_DIM = 64
ATT_KV_HEADS = 4
WINDOW = 128
ROT_DIM = HEAD_DIM // 4
ROPE_THETA = 500000.0
RWKV_HEAD_DIM = 64
DECAY_LORA = 64
AAA_LORA = 64
GATE_LORA = 128
RWKV_GN_EPS = 64e-5
CROSS_HEADS = 4
PEER_HEADS = 8
PEER_TOPK = 16

LANES = 128
SUBLANES = 8
VMEM_LIMIT = 56 * 1024 * 1024


def _cparams(*sem):
    return pltpu.CompilerParams(dimension_semantics=sem, vmem_limit_bytes=VMEM_LIMIT)


def _const_spec(shape):
    nd = len(shape)
    return pl.BlockSpec(shape, lambda *_: (0,) * nd, pipeline_mode=pl.Buffered(1))


def _rms(x, g):
    return x * lax.rsqrt(jnp.mean(x * x, axis=-1, keepdims=True) + RMS_EPS) * g


def _sigmoid(x):
    return 1.0 / (1.0 + jnp.exp(-x))


def _rope(t, c, s):
    w = t.shape[-1]
    rep = w // LANES
    cc = jnp.tile(c, (1, rep))
    ss = jnp.tile(s, (1, rep))
    lane = lax.broadcasted_iota(jnp.int32, t.shape, 1)
    first = (lane % HEAD_DIM) < (ROT_DIM // 2)
    partner = jnp.where(first, pltpu.roll(t, w - ROT_DIM // 2, axis=1), pltpu.roll(t, ROT_DIM // 2, axis=1))
    return t * cc + partner * ss


def _inproj_kernel(x_ref, g_ref, w_ref, c_ref, s_ref, q_ref, k_ref, v_ref, p_ref, gate_ref, *, widths):
    wq, wk, wv, wp, wg = widths
    xb = _rms(x_ref[...], g_ref[...]).astype(BF16)
    c = c_ref[...]
    s = s_ref[...]
    o = 0
    q = jnp.dot(xb, w_ref[:, o:o + wq], preferred_element_type=F32)
    q_ref[...] = _rope(q, c, s).astype(q_ref.dtype)
    o += wq
    k = jnp.dot(xb, w_ref[:, o:o + wk], preferred_element_type=F32)
    k_ref[...] = _rope(k, c, s).astype(k_ref.dtype)
    o += wk
    v_ref[...] = jnp.dot(xb, w_ref[:, o:o + wv], preferred_element_type=F32).astype(v_ref.dtype)
    o += wv
    p_ref[...] = jnp.dot(xb, w_ref[:, o:o + wp], preferred_element_type=F32)
    o += wp
    gate_ref[...] = _sigmoid(jnp.dot(xb, w_ref[:, o:o + wg], preferred_element_type=F32)).astype(gate_ref.dtype)


def _in_proj(x2, g_mix, w_in_b, rope_c, rope_s, widths, tm):
    n, d = x2.shape
    wq, wk, wv, wp, wg = widths
    row = lambda w: pl.BlockSpec((tm, w), lambda i: (i, 0))
    return pl.pallas_call(
        functools.partial(_inproj_kernel, widths=widths),
        grid=(n // tm,),
        in_specs=[row(d), _const_spec((1, d)), _const_spec(w_in_b.shape), row(LANES), row(LANES)],
        out_specs=[row(wq), row(wk), row(wv), row(wp), row(wg)],
        out_shape=[jax.ShapeDtypeStruct((n, wq), BF16), jax.ShapeDtypeStruct((n, wk), BF16),
                   jax.ShapeDtypeStruct((n, wv), BF16), jax.ShapeDtypeStruct((n, wp), F32),
                   jax.ShapeDtypeStruct((n, wg), BF16)],
        compiler_params=_cparams("parallel"),
        name="in_proj",
    )(x2, g_mix.reshape(1, d), w_in_b, rope_c, rope_s)


def _rope_tables(positions):
    half = ROT_DIM // 2
    inv_freq = 1.0 / (ROPE_THETA ** (jnp.arange(0, ROT_DIM, 2, dtype=F32) / ROT_DIM))
    ang = positions.reshape(-1).astype(F32)[:, None] * inv_freq
    cos, sin = jnp.cos(ang), jnp.sin(ang)
    n = ang.shape[0]
    pad1 = jnp.ones((n, HEAD_DIM - ROT_DIM), F32)
    pad0 = jnp.zeros((n, HEAD_DIM - ROT_DIM), F32)
    c = jnp.concatenate([cos, cos, pad1], axis=1)
    s = jnp.concatenate([-sin, sin, pad0], axis=1)
    return jnp.tile(c, (1, LANES // HEAD_DIM)), jnp.tile(s, (1, LANES // HEAD_DIM))


def _swa_kernel(sink_ref, q_ref, kp_ref, kc_ref, vp_ref, vc_ref, o_ref, *, group):
    n = pl.program_id(1)
    blk = q_ref.shape[0]
    scale = HEAD_DIM ** -0.5
    rows = group * blk
    qi = lax.broadcasted_iota(jnp.int32, (rows, 2 * blk), 0) % blk
    kj = lax.broadcasted_iota(jnp.int32, (rows, 2 * blk), 1)
    valid = (kj > qi) & (kj <= qi + WINDOW) & ((kj >= blk) | (n > 0))
    rid = lax.broadcasted_iota(jnp.int32, (rows, 1), 0) // blk
    KV = range(ATT_KV_HEADS)
    sls = [slice(kv * HEAD_DIM, (kv + 1) * HEAD_DIM) for kv in KV]
    heads = [[kv * group + g for g in range(group)] for kv in KV]
    s = []
    for kv in KV:
        kk = jnp.concatenate([kp_ref[:, sls[kv]], kc_ref[:, sls[kv]]], axis=0)
        qq = jnp.concatenate([q_ref[:, h * HEAD_DIM:(h + 1) * HEAD_DIM] for h in heads[kv]], axis=0)
        s.append(lax.dot_general(qq, kk, (((1,), (1,)), ((), ())), preferred_element_type=F32))
    p, inv = [], []
    for kv in KV:
        sc = jnp.where(valid, s[kv] * scale, -1e30)
        sink = jnp.zeros((rows, 1), F32)
        for g, h in enumerate(heads[kv]):
            sink = jnp.where(rid == g, sink_ref[h], sink)
        m = jnp.maximum(jnp.max(sc, axis=-1, keepdims=True), sink)
        e = jnp.exp(sc - m)
        inv.append(1.0 / (jnp.sum(e, axis=-1, keepdims=True) + jnp.exp(sink - m)))
        p.append(e.astype(BF16))
    for kv in KV:
        vv = jnp.concatenate([vp_ref[:, sls[kv]], vc_ref[:, sls[kv]]], axis=0)
        o = jnp.dot(p[kv], vv, preferred_element_type=F32) * inv[kv]
        for g, h in enumerate(heads[kv]):
            o_ref[:, h * HEAD_DIM:(h + 1) * HEAD_DIM] = o[g * blk:(g + 1) * blk].astype(o_ref.dtype)


def _swa(q, k, v, sinks, batch, seq):
    n, wq = q.shape
    wk = k.shape[1]
    blk = WINDOW
    nb = seq // blk
    group = (wq // HEAD_DIM) // ATT_KV_HEADS
    cur = lambda w: pl.BlockSpec((blk, w), lambda b, i: (b * nb + i, 0))
    prev = lambda w: pl.BlockSpec((blk, w), lambda b, i: (b * nb + jnp.maximum(i - 1, 0), 0))
    return pl.pallas_call(
        functools.partial(_swa_kernel, group=group),
        grid=(batch, nb),
        in_specs=[pl.BlockSpec(memory_space=pltpu.SMEM), cur(wq), prev(wk), cur(wk), prev(wk), cur(wk)],
        out_specs=cur(wq),
        out_shape=jax.ShapeDtypeStruct((n, wq), BF16),
        compiler_params=_cparams("parallel", "parallel"),
        name="swa",
    )(sinks.astype(F32), q, k, k, v, v)


RW_CHUNK = 64
RW_SEQS = 2


def _mm(a, b):
    return jnp.dot(a.astype(BF16), b.astype(BF16), preferred_element_type=F32)


def _mm_nt(a, b):
    return lax.dot_general(a.astype(BF16), b.astype(BF16), (((1,), (1,)), ((), ())), preferred_element_type=F32)


def _mm_tn(a, b):
    return lax.dot_general(a.astype(BF16), b.astype(BF16), (((0,), (0,)), ((), ())), preferred_element_type=F32)


def _hi_lo(a):
    hi = a.astype(BF16)
    return hi, (a - hi.astype(F32)).astype(BF16)


def _split_dot(a, b01):
    hi, lo = _hi_lo(a)
    return jnp.dot(hi, b01, preferred_element_type=F32) + jnp.dot(lo, b01, preferred_element_type=F32)


def _split_dot_left(a01, b):
    hi, lo = _hi_lo(b)
    return jnp.dot(a01, hi, preferred_element_type=F32) + jnp.dot(a01, lo, preferred_element_type=F32)


def _rwkv_kernel(p_ref, mu_ref, w0_ref, w2_ref, a0_ref, a2_ref, g2_ref, kk_ref, ka_ref, rk_ref, lnw_ref, lnb_ref,
                 o_ref, last_ref, st_ref, *, width):
    c = pl.program_id(1)
    NB, C, wp = p_ref.shape
    R = NB * C
    W = width
    N = RWKV_HEAD_DIM
    SL = 2 * N
    n_slab = W // SL

    @pl.when(c == 0)
    def _():
        last_ref[...] = jnp.zeros_like(last_ref)
        st_ref[...] = jnp.zeros_like(st_ref)

    p = p_ref[...].reshape(R, wp)
    row = lax.broadcasted_iota(jnp.int32, (R, 1), 0)
    carry = jnp.concatenate([jnp.broadcast_to(last_ref[b:b + 1, :], (C, wp)) for b in range(NB)], axis=0)
    prev = jnp.where(row % C == 0, carry, pltpu.roll(p, 1, axis=0))
    for b in range(NB):
        last_ref[b:b + 1, :] = p[(b + 1) * C - 1:(b + 1) * C, :]
    p = p + (prev - p) * mu_ref[...]
    r = p[:, :W]
    k = p[:, W:2 * W]
    v = p[:, 2 * W:3 * W]
    o3 = 3 * W
    dw = p[:, o3:o3 + DECAY_LORA]
    da = p[:, o3 + DECAY_LORA:o3 + DECAY_LORA + AAA_LORA]
    dg = p[:, o3 + DECAY_LORA + AAA_LORA:]

    z = -(w0_ref[...] + _mm(jnp.tanh(dw), w2_ref[...]))
    w_log = -(jnp.maximum(z, 0.0) + jnp.log(1.0 + jnp.exp(-jnp.abs(z)))) - 0.5
    lw = -jnp.exp(w_log)
    a = _sigmoid(a0_ref[...] + _mm(da, a2_ref[...]))
    g = _mm(_sigmoid(dg), g2_ref[...])

    ti = lax.broadcasted_iota(jnp.int32, (R, R), 0)
    tj = lax.broadcasted_iota(jnp.int32, (R, R), 1)
    tri = ((ti >= tj) & (ti // C == tj // C)).astype(BF16)
    cum = _split_dot_left(tri, lw)
    e_pos = jnp.exp(cum)
    e_neg = jnp.exp(-cum)
    e_prev = jnp.exp(cum - lw)

    li = lax.broadcasted_iota(jnp.int32, (SL, SL), 0)
    lj = lax.broadcasted_iota(jnp.int32, (SL, SL), 1)
    same = (li // N) == (lj // N)
    head_sum = same.astype(BF16)
    strict = same & ((li % N) > (lj % N))
    incl = same & ((li % N) >= (lj % N))
    lane = lax.broadcasted_iota(jnp.int32, (C, SL), 1)
    m0 = lane < N

    def stack(t):
        return jnp.concatenate([jnp.where(m0, t, 0.0), jnp.where(m0, 0.0, t)], axis=0)

    items = [(b, s) for b in range(NB) for s in range(n_slab)]
    S = range(len(items))
    at = [(slice(b * C, (b + 1) * C), slice(s * SL, (s + 1) * SL)) for b, s in items]
    lanes = [sl for _, sl in at]
    h2 = 2 * C
    bf = lambda t: t.astype(BF16)
    e_end = [jnp.exp(cum[rw.stop - 1:rw.stop, sl] - cum[rw, sl]) for rw, sl in at]
    x = [k[i] * kk_ref[:, i[1]] for i in at]
    k2 = [k[i] * (1.0 + (a[i] - 1.0) * ka_ref[:, i[1]]) for i in at]
    sums = [_split_dot(jnp.concatenate([x[s] * x[s], r[at[s]] * k2[s] * rk_ref[:, lanes[s]]], axis=0), head_sum)
            for s in S]
    kk = [x[s] / jnp.maximum(jnp.sqrt(sums[s][:C]), 1e-12) for s in S]
    kb = [kk[s] * a[at[s]] for s in S]
    AR = [bf(jnp.concatenate([stack(-kk[s] * e_prev[at[s]]), stack(r[at[s]] * e_pos[at[s]])], axis=0))
          for s in S]
    twice = lambda t: jnp.concatenate([t, t], axis=0)
    BK = [bf(jnp.concatenate([twice(kb[s] * e_neg[at[s]]), twice(k2[s] * e_neg[at[s]])], axis=0)) for s in S]
    V2 = [bf(stack(v[i])) for i in at]
    Be = [bf(twice(kb[s] * e_end[s])) for s in S]
    Ke = [bf(twice(k2[s] * e_end[s])) for s in S]
    st = [st_ref[s] for s in S]
    G = [_mm_nt(AR[s], BK[s]) for s in S]
    ST = [_mm(AR[s], st[s]) for s in S]
    A_k = [bf(jnp.concatenate([jnp.where(strict, G[s][:h2, h2:], 0.0), jnp.where(incl, G[s][h2:, h2:], 0.0)],
                              axis=0)) for s in S]
    AV = [_mm(A_k[s], V2[s]) for s in S]
    U = [ST[s][:h2] + AV[s][:h2] for s in S]
    Ap = [bf(jnp.where(strict, G[s][:h2, :h2], 0.0)) for s in S]
    steps = int(math.log2(C))
    for i in range(steps - 1):
        Rm = [_mm(Ap[s], jnp.concatenate([Ap[s], bf(U[s])], axis=1)) for s in S]
        Ap = [bf(Rm[s][:, :SL]) for s in S]
        U = [U[s] + Rm[s][:, SL:] for s in S]
    U = [U[s] + _mm(Ap[s], U[s]) for s in S]
    Ub = [bf(U[s]) for s in S]
    A_rb = [bf(jnp.where(incl, G[s][h2:, :h2], 0.0)) for s in S]
    O2 = [ST[s][h2:] + AV[s][h2:] + _mm(A_rb[s], Ub[s]) for s in S]
    for s in S:
        rw, sl = at[s]
        decay = jnp.transpose(jnp.broadcast_to(e_pos[rw.stop - 1:rw.stop, sl], (SL, SL)))
        st_new = st[s] * decay + _mm_tn(Be[s], Ub[s]) + _mm_tn(Ke[s], V2[s])
        st_ref[s] = jnp.where(same, st_new, 0.0)
    o = [O2[s][:C] + O2[s][C:] for s in S]
    mean = [_split_dot(o[s], head_sum) * (1.0 / N) for s in S]
    dlt = [o[s] - mean[s] for s in S]
    var = [_split_dot(dlt[s] * dlt[s], head_sum) * (1.0 / N) for s in S]
    bonus = [sums[s][C:] * v[at[s]] for s in S]
    for s in S:
        (b, _), sl = items[s], lanes[s]
        on = dlt[s] * lax.rsqrt(var[s] + RWKV_GN_EPS) * lnw_ref[:, sl] + lnb_ref[:, sl]
        o_ref[b, :, sl] = ((on + bonus[s]) * g[at[s]]).astype(o_ref.dtype)


def _rwkv(p, mu, w0, w2, a0, a2, g2, k_k, k_a, r_k, ln_w, ln_b, batch, seq):
    n, wp = p.shape
    width = w0.shape[-1]
    C = RW_CHUNK
    nb = RW_SEQS
    vec = lambda t: t.reshape(1, -1).astype(F32)
    args = [vec(mu), vec(w0), w2.astype(BF16), vec(a0), a2.astype(BF16), g2.astype(BF16), vec(k_k), vec(k_a),
            vec(r_k), vec(ln_w), vec(ln_b)]
    out = pl.pallas_call(
        functools.partial(_rwkv_kernel, width=width),
        grid=(batch // nb, seq // C),
        in_specs=[pl.BlockSpec((nb, C, wp), lambda b, c: (b, c, 0))] + [_const_spec(t.shape) for t in args],
        out_specs=pl.BlockSpec((nb, C, width), lambda b, c: (b, c, 0)),
        out_shape=jax.ShapeDtypeStruct((batch, seq, width), BF16),
        scratch_shapes=[pltpu.VMEM((nb, wp), F32),
                        pltpu.VMEM((nb * width // (2 * RWKV_HEAD_DIM), 2 * RWKV_HEAD_DIM, 2 * RWKV_HEAD_DIM), F32)],
        compiler_params=_cparams("parallel", "arbitrary"),
        name="rwkv",
    )(p.reshape(batch, seq, wp), *args)
    return out.reshape(n, width)


def _memkv_kernel(m_ref, g_ref, w_ref, o_ref):
    o_ref[...] = jnp.dot(_rms(m_ref[...], g_ref[...]).astype(BF16), w_ref[...],
                         preferred_element_type=F32).astype(o_ref.dtype)


def _mem_kv(mem2, g_mem, w_kv_b, mem_len):
    n, d = mem2.shape
    wo = w_kv_b.shape[1]
    return pl.pallas_call(
        _memkv_kernel,
        grid=(n // mem_len,),
        in_specs=[pl.BlockSpec((mem_len, d), lambda i: (i, 0)), _const_spec((1, d)), _const_spec(w_kv_b.shape)],
        out_specs=pl.BlockSpec((mem_len, wo), lambda i: (i, 0)),
        out_shape=jax.ShapeDtypeStruct((n, wo), BF16),
        compiler_params=_cparams("parallel"),
        name="mem_kv",
    )(mem2, g_mem.reshape(1, d), w_kv_b)


def _post_kernel(x_ref, at_ref, rw_ref, gt_ref, kv_ref, wo_ref, gc_ref, wq_ref, woc_ref, gf_ref, wpq_ref,
                 k1_ref, k2_ref, h_ref, s1_ref, s2_ref):
    d = x_ref.shape[1]
    ga = gt_ref[:, :d].astype(F32)
    gb = gt_ref[:, d:].astype(F32)
    mixed = ga * at_ref[...].astype(F32) + gb * rw_ref[...].astype(F32)
    h1 = x_ref[...] + jnp.dot(mixed.astype(BF16), wo_ref[...], preferred_element_type=F32)

    qc = jnp.dot(_rms(h1, gc_ref[...]).astype(BF16), wq_ref[...], preferred_element_type=F32)
    wc = qc.shape[1]
    hd = wc // CROSS_HEADS
    scale = hd ** -0.5
    H = range(CROSS_HEADS)
    qb = (qc * scale).astype(BF16)
    s = [lax.dot_general(qb[:, hh * hd:(hh + 1) * hd], kv_ref[:, hh * hd:(hh + 1) * hd],
                         (((1,), (1,)), ((), ())), preferred_element_type=F32) for hh in H]
    ex = [jnp.exp(s[hh] - jnp.max(s[hh], axis=-1, keepdims=True)) for hh in H]
    inv = [1.0 / jnp.sum(ex[hh], axis=-1, keepdims=True) for hh in H]
    outs = [jnp.dot(ex[hh].astype(BF16), kv_ref[:, wc + hh * hd:wc + (hh + 1) * hd],
                    preferred_element_type=F32) * inv[hh] for hh in H]
    oc = jnp.concatenate(outs, axis=1)
    h2 = h1 + jnp.dot(oc.astype(BF16), woc_ref[...], preferred_element_type=F32)
    h_ref[...] = h2

    q3 = jnp.dot(_rms(h2, gf_ref[...]).astype(BF16), wpq_ref[...], preferred_element_type=F32)
    half = k1_ref.shape[1]
    dn = (((1,), (1,)), ((), ()))
    q_hi, q_lo = _hi_lo(q3)
    k_parts = [_hi_lo(k1_ref[...]), _hi_lo(k2_ref[...])]

    def score(kp, c0):
        qh, ql = q_hi[:, c0:c0 + half], q_lo[:, c0:c0 + half]
        return (lax.dot_general(kp[0], qh, dn, preferred_element_type=F32)
                + lax.dot_general(kp[0], ql, dn, preferred_element_type=F32)
                + lax.dot_general(kp[1], qh, dn, preferred_element_type=F32))

    for hh in range(PEER_HEADS):
        s1_ref[hh] = score(k_parts[0], 2 * hh * half)
        s2_ref[hh] = score(k_parts[1], (2 * hh + 1) * half)


def _post(x2, attn, rw, gates, kv, w_out_b, g_cross, w_qc_b, w_oc_b, g_ffn, w_pq_b, k1, k2, batch, seq, mem_len, tm):
    n, d = x2.shape
    nt = seq // tm
    n_keys = k1.shape[0]
    row = lambda w: pl.BlockSpec((tm, w), lambda b, i: (b * nt + i, 0))
    sc_spec = pl.BlockSpec((PEER_HEADS, n_keys, tm), lambda b, i: (0, 0, b * nt + i))
    consts = [w_out_b, g_cross.reshape(1, d), w_qc_b, w_oc_b, g_ffn.reshape(1, d), w_pq_b, k1.astype(F32), k2.astype(F32)]
    return pl.pallas_call(
        _post_kernel,
        grid=(batch, nt),
        in_specs=[row(d), row(d), row(d), row(2 * d), pl.BlockSpec((mem_len, kv.shape[1]), lambda b, i: (b, 0))]
                 + [_const_spec(t.shape) for t in consts],
        out_specs=[row(d), sc_spec, sc_spec],
        out_shape=[jax.ShapeDtypeStruct((n, d), F32), jax.ShapeDtypeStruct((PEER_HEADS, n_keys, n), F32),
                   jax.ShapeDtypeStruct((PEER_HEADS, n_keys, n), F32)],
        compiler_params=_cparams("parallel", "parallel"),
        name="post",
    )(x2, attn, rw, gates, kv, *consts)


ROUTE_UNROLL = 4
TABLE_ROWS_PER_EXPERT = 4


def _topk_rows(s, k, payload=None):
    rows = s.shape[0]
    rid = lax.broadcasted_iota(jnp.int32, s.shape, 0).astype(F32)
    vals, outs = [], []
    for _ in range(k):
        m = jnp.max(s, axis=0, keepdims=True)
        i = jnp.min(jnp.where(s == m, rid, float(rows)), axis=0, keepdims=True)
        hit = rid == i
        vals.append(m)
        outs.append(i if payload is None else jnp.max(jnp.where(hit, payload, -1.0), axis=0, keepdims=True))
        s = jnp.where(hit, -jnp.inf, s)
    return jnp.concatenate(vals, axis=0), jnp.concatenate(outs, axis=0)


def _route_kernel(s1_ref, s2_ref, e_ref, g_ref):
    K = PEER_TOPK
    n_keys = s1_ref.shape[1]

    def one_head(h):
        v1, i1 = _topk_rows(s1_ref[h], K)
        v2, i2 = _topk_rows(s2_ref[h], K)
        cand, ecand = [], []
        for a in range(K):
            nb = K // (a + 1)
            cand.append(v1[a:a + 1, :] + v2[:nb, :])
            ecand.append(i1[a:a + 1, :] * float(n_keys) + i2[:nb, :])
        n_c = sum(c.shape[0] for c in cand)
        pad = -n_c % SUBLANES
        if pad:
            cand.append(jnp.full((pad, v1.shape[1]), -jnp.inf, F32))
            ecand.append(jnp.full((pad, v1.shape[1]), -1.0, F32))
        sc, e = _topk_rows(jnp.concatenate(cand, axis=0), K, jnp.concatenate(ecand, axis=0))
        e_ref[h] = (e * float(TABLE_ROWS_PER_EXPERT)).astype(jnp.int32)
        ex = jnp.exp(sc - sc[0:1, :])
        g_ref[h] = ex / jnp.sum(ex, axis=0, keepdims=True)

    def body(i, carry):
        for j in range(ROUTE_UNROLL):
            one_head(i * ROUTE_UNROLL + j)
        return carry

    lax.fori_loop(0, s1_ref.shape[0] // ROUTE_UNROLL, body, 0)


def _route(s1, s2, tk):
    heads, n_keys, n = s1.shape
    spec_in = pl.BlockSpec((heads, n_keys, tk), lambda i: (0, 0, i))
    spec_out = pl.BlockSpec((heads, PEER_TOPK, tk), lambda i: (0, 0, i))
    return pl.pallas_call(
        _route_kernel,
        grid=(n // tk,),
        in_specs=[spec_in, spec_in],
        out_specs=[spec_out, spec_out],
        out_shape=[jax.ShapeDtypeStruct((heads, PEER_TOPK, n), jnp.int32),
                   jax.ShapeDtypeStruct((heads, PEER_TOPK, n), F32)],
        compiler_params=_cparams("parallel"),
        name="route",
    )(s1, s2)


PEER_GROUP = 16
SUB = SUBLANES
PEER_UNROLL = 32


def _gather_rows(tbl_ref, e_ref, t, g):
    rows = TABLE_ROWS_PER_EXPERT
    parts = [tbl_ref[pl.ds(pl.multiple_of(e_ref.at[g * PEER_GROUP + j][t], rows), rows), :]
             for j in range(PEER_GROUP)]
    return pltpu.bitcast(jnp.concatenate(parts, axis=0), BF16)


def _diag_mask(shape):
    r = lax.broadcasted_iota(jnp.int32, shape, len(shape) - 2)
    c = lax.broadcasted_iota(jnp.int32, shape, len(shape) - 1)
    return r == (c % SUB)


def _with_index_tile(e_hbm, bufs, sems, tm, fn):
    i = pl.program_id(0)

    def copy(step, slot):
        return pltpu.make_async_copy(e_hbm.at[:, pl.ds(pl.multiple_of(step * tm, tm), tm)], bufs[slot], sems.at[slot])

    @pl.when(i == 0)
    def _():
        copy(0, 0).start()

    for slot in range(2):
        @pl.when(i % 2 == slot)
        def _(slot=slot):
            @pl.when(i + 1 < pl.num_programs(0))
            def _():
                copy(i + 1, 1 - slot).start()

            copy(i, slot).wait()
            fn(bufs[slot])


def _peer_act_kernel(e_hbm, h_ref, gf_ref, gate_ref, tbl_ref, o_ref, act_ref, x2_ref, e_a, e_b, sems):
    tm = h_ref.shape[0]
    n_grp = e_a.shape[0] // PEER_GROUP
    xn = _rms(h_ref[...], gf_ref[...]).reshape(tm, SUB, LANES).astype(BF16)
    x2_ref[...] = jnp.concatenate([xn, xn], axis=1)
    kr = lax.broadcasted_iota(jnp.int32, (PEER_GROUP, PEER_GROUP * SUB), 0)
    kc = lax.broadcasted_iota(jnp.int32, (PEER_GROUP, PEER_GROUP * SUB), 1) // SUB
    row_sum = (kr == kc).astype(BF16)
    sub = lax.broadcasted_iota(jnp.int32, (SUB, LANES), 0)

    def gather_loop(e_ref):
        def body(i, carry):
            qs = []
            for j in range(PEER_UNROLL):
                t = i * PEER_UNROLL + j
                xt = jnp.tile(x2_ref[t], (PEER_GROUP // 2, 1))
                qs.append(jnp.concatenate([jnp.dot(row_sum, _gather_rows(tbl_ref, e_ref, t, g) * xt,
                                                   preferred_element_type=F32) for g in range(n_grp)], axis=0))
            for jb in range(0, PEER_UNROLL, SUB):
                blk = jnp.zeros((SUB, LANES), F32)
                for j in range(SUB):
                    dots = jnp.sum(qs[jb + j].T, axis=0, keepdims=True)
                    blk = jnp.where(sub == j, dots, blk)
                act_ref[pl.ds(pl.multiple_of(i * PEER_UNROLL + jb, SUB), SUB), :] = blk
            return carry

        lax.fori_loop(0, tm // PEER_UNROLL, body, 0)

    _with_index_tile(e_hbm, (e_a, e_b), sems, tm, gather_loop)
    act = act_ref[...]
    gelu = 0.5 * act * (1.0 + lax.erf(act * (2.0 ** -0.5)))
    o_ref[...] = gate_ref[...].T * gelu


def _peer_out_kernel(e_hbm, h_ref, gw_ref, tbl_ref, exp_ref, gfin_ref, o_ref, ge_ref, acc_ref, e_a, e_b, sems):
    tm = h_ref.shape[0]
    n_grp = e_a.shape[0] // PEER_GROUP
    ge_ref[...] = _split_dot(gw_ref[...], exp_ref[...])
    mask = _diag_mask((SUB, n_grp * LANES))

    def gather_loop(e_ref):
        def body(i, carry):
            for j in range(PEER_UNROLL):
                t = i * PEER_UNROLL + j
                lhs = jnp.where(mask, jnp.broadcast_to(ge_ref[pl.ds(t, 1), :], mask.shape), 0.0).astype(BF16)
                acc = jnp.zeros((SUB, LANES), F32)
                for g in range(n_grp):
                    w = _gather_rows(tbl_ref, e_ref, t, g)
                    acc = acc + jnp.dot(lhs[:, g * LANES:(g + 1) * LANES], w, preferred_element_type=F32)
                acc_ref[t] = acc
            return carry

        lax.fori_loop(0, tm // PEER_UNROLL, body, 0)

    _with_index_tile(e_hbm, (e_a, e_b), sems, tm, gather_loop)
    o_ref[...] = _rms(h_ref[...] + acc_ref[...].reshape(o_ref.shape), gfin_ref[...])


def _table_tiles(t):
    e, d = t.shape
    p = d // (2 * LANES)
    te = 512
    return pl.pallas_call(
        functools.partial(_pack_kernel, rows=p),
        grid=(e // te,),
        in_specs=[pl.BlockSpec((te, d), lambda i: (i, 0))],
        out_specs=pl.BlockSpec((te * p, LANES), lambda i: (i, 0)),
        out_shape=jax.ShapeDtypeStruct((e * p, LANES), jnp.int32),
        compiler_params=_cparams("parallel"),
        name="pack_table",
    )(t)


def _pack_kernel(t_ref, o_ref, *, rows):
    te = t_ref.shape[0]
    for p in range(rows):
        halves = []
        for c in range(2):
            x = t_ref[:, (2 * p + c) * LANES:(2 * p + c + 1) * LANES]
            halves.append(pltpu.bitcast(x.astype(BF16).astype(F32), jnp.int32))
        word = lax.shift_right_logical(halves[0], 16) | halves[1]
        o_ref[pl.ds(p, te, stride=rows), :] = word


def _peer(h2, e_idx, gate, g_ffn, g_final, u_tiles, v_tiles, tm):
    n, d = h2.shape
    kk = e_idx.shape[0]
    rows = pl.BlockSpec((tm, d), lambda i: (i, 0))
    row = pl.BlockSpec((tm, kk), lambda i: (i, 0))
    lane = jnp.arange(kk * SUB) // SUB
    expand = (jnp.arange(kk)[:, None] == lane[None, :]).astype(BF16)
    index_scratch = [pltpu.SMEM((kk, tm), jnp.int32), pltpu.SMEM((kk, tm), jnp.int32), pltpu.SemaphoreType.DMA((2,))]
    gw = pl.pallas_call(
        _peer_act_kernel,
        grid=(n // tm,),
        in_specs=[pl.BlockSpec(memory_space=pl.ANY), rows, _const_spec((1, d)),
                  pl.BlockSpec((kk, tm), lambda i: (0, i)), _const_spec(u_tiles.shape)],
        out_specs=row,
        out_shape=jax.ShapeDtypeStruct((n, kk), F32),
        scratch_shapes=[pltpu.VMEM((tm, kk), F32), pltpu.VMEM((tm, 2 * SUB, d // SUB), BF16)] + index_scratch,
        compiler_params=_cparams("arbitrary"),
        name="peer_act",
    )(e_idx, h2, g_ffn.reshape(1, d), gate, u_tiles)
    return pl.pallas_call(
        _peer_out_kernel,
        grid=(n // tm,),
        in_specs=[pl.BlockSpec(memory_space=pl.ANY), rows, row, _const_spec(v_tiles.shape), _const_spec(expand.shape),
                  _const_spec((1, d))],
        out_specs=rows,
        out_shape=jax.ShapeDtypeStruct((n, d), F32),
        scratch_shapes=[pltpu.VMEM((tm, kk * SUB), F32), pltpu.VMEM((tm, SUB, d // SUB), F32)] + index_scratch,
        compiler_params=_cparams("arbitrary"),
        name="peer_out",
    )(e_idx, h2, gw, v_tiles, expand, g_final.reshape(1, d))


def kernel(x, mem, positions, g_mix, w_in, attn_sinks, rwkv_mu, rwkv_w0, rwkv_w2, rwkv_a0, rwkv_a2, rwkv_g2, rwkv_k_k, rwkv_k_a, rwkv_r_k, rwkv_ln_w, rwkv_ln_b, w_out, g_cross, g_mem, w_q_cross, w_kv_cross, w_o_cross, g_ffn, peer_w_q, peer_sub_keys_1, peer_sub_keys_2, peer_u, peer_v, g_final):
    b, s, d = x.shape
    n = b * s
    L = 0
    wq = d
    wk = ATT_KV_HEADS * HEAD_DIM
    wp = 3 * d + DECAY_LORA + AAA_LORA + GATE_LORA
    widths = (wq, wk, wk, wp, 2 * d)
    rc, rs = _rope_tables(positions)
    q, k, v, p, gates = _in_proj(x.reshape(n, d), g_mix[L], w_in[L].astype(BF16), rc, rs, widths, 256)
    attn = _swa(q, k, v, attn_sinks[L], b, s)
    rw = _rwkv(p, rwkv_mu[L], rwkv_w0[L], rwkv_w2[L], rwkv_a0[L], rwkv_a2[L], rwkv_g2[L], rwkv_k_k[L], rwkv_k_a[L],
               rwkv_r_k[L], rwkv_ln_w[L], rwkv_ln_b[L], b, s)
    m = mem.shape[1]
    kv = _mem_kv(mem.reshape(b * m, d), g_mem[L], w_kv_cross[L].astype(BF16), m)
    h2, s1, s2 = _post(x.reshape(n, d), attn, rw, gates, kv, w_out[L].astype(BF16), g_cross[L],
                       w_q_cross[L].astype(BF16), w_o_cross[L].astype(BF16), g_ffn[L], peer_w_q[L].astype(BF16),
                       peer_sub_keys_1[L], peer_sub_keys_2[L], b, s, m, 256)
    e_t, g_t = _route(s1, s2, LANES)
    kk = PEER_HEADS * PEER_TOPK
    assert d == TABLE_ROWS_PER_EXPERT * 2 * LANES
    out = _peer(h2, e_t.reshape(kk, n), g_t.reshape(kk, n), g_ffn[L], g_final, _table_tiles(peer_u[L]),
                _table_tiles(peer_v[L]), LANES)
    return out.reshape(b, s, d)
```

```python
import functools
import math

import jax
import jax.numpy as jnp
from jax import lax
from jax.experimental import pallas as pl
from jax.experimental.pallas import tpu as pltpu

F32 = jnp.float32
BF16 = jnp.bfloat16

RMS_EPS = 1e-5
HEAD_DIM = 64
ATT_KV_HEADS = 4
WINDOW = 128
ROT_DIM = HEAD_DIM // 4
ROPE_THETA = 500000.0
RWKV_HEAD_DIM = 64
DECAY_LORA = 64
AAA_LORA = 64
GATE_LORA = 128
RWKV_GN_EPS = 64e-5
CROSS_HEADS = 4
PEER_HEADS = 8
PEER_TOPK = 16

LANES = 128
SUBLANES = 8
VMEM_LIMIT = 56 * 1024 * 1024


def _cparams(*sem):
    return pltpu.CompilerParams(dimension_semantics=sem, vmem_limit_bytes=VMEM_LIMIT)


def _const_spec(shape):
    nd = len(shape)
    return pl.BlockSpec(shape, lambda *_: (0,) * nd, pipeline_mode=pl.Buffered(1))


def _rms(x, g):
    return x * lax.rsqrt(jnp.mean(x * x, axis=-1, keepdims=True) + RMS_EPS) * g


def _sigmoid(x):
    return 1.0 / (1.0 + jnp.exp(-x))


def _rope(t, c, s):
    w = t.shape[-1]
    rep = w // LANES
    cc = jnp.tile(c, (1, rep))
    ss = jnp.tile(s, (1, rep))
    lane = lax.broadcasted_iota(jnp.int32, t.shape, 1)
    first = (lane % HEAD_DIM) < (ROT_DIM // 2)
    partner = jnp.where(first, pltpu.roll(t, w - ROT_DIM // 2, axis=1), pltpu.roll(t, ROT_DIM // 2, axis=1))
    return t * cc + partner * ss


def _inproj_kernel(x_ref, g_ref, w_ref, c_ref, s_ref, q_ref, k_ref, v_ref, p_ref, gate_ref, *, widths):
    wq, wk, wv, wp, wg = widths
    xb = _rms(x_ref[...], g_ref[...]).astype(BF16)
    c = c_ref[...]
    s = s_ref[...]
    o = 0
    q = jnp.dot(xb, w_ref[:, o:o + wq], preferred_element_type=F32)
    q_ref[...] = _rope(q, c, s).astype(q_ref.dtype)
    o += wq
    k = jnp.dot(xb, w_ref[:, o:o + wk], preferred_element_type=F32)
    k_ref[...] = _rope(k, c, s).astype(k_ref.dtype)
    o += wk
    v_ref[...] = jnp.dot(xb, w_ref[:, o:o + wv], preferred_element_type=F32).astype(v_ref.dtype)
    o += wv
    p_ref[...] = jnp.dot(xb, w_ref[:, o:o + wp], preferred_element_type=F32)
    o += wp
    gate_ref[...] = _sigmoid(jnp.dot(xb, w_ref[:, o:o + wg], preferred_element_type=F32)).astype(gate_ref.dtype)


def _in_proj(x2, g_mix, w_in_b, rope_c, rope_s, widths, tm):
    n, d = x2.shape
    wq, wk, wv, wp, wg = widths
    row = lambda w: pl.BlockSpec((tm, w), lambda i: (i, 0))
    return pl.pallas_call(
        functools.partial(_inproj_kernel, widths=widths),
        grid=(n // tm,),
        in_specs=[row(d), _const_spec((1, d)), _const_spec(w_in_b.shape), row(LANES), row(LANES)],
        out_specs=[row(wq), row(wk), row(wv), row(wp), row(wg)],
        out_shape=[jax.ShapeDtypeStruct((n, wq), BF16), jax.ShapeDtypeStruct((n, wk), BF16),
                   jax.ShapeDtypeStruct((n, wv), BF16), jax.ShapeDtypeStruct((n, wp), F32),
                   jax.ShapeDtypeStruct((n, wg), BF16)],
        compiler_params=_cparams("parallel"),
        name="in_proj",
    )(x2, g_mix.reshape(1, d), w_in_b, rope_c, rope_s)


def _rope_tables(positions):
    inv_freq = 1.0 / (ROPE_THETA ** (jnp.arange(0, ROT_DIM, 2, dtype=F32) / ROT_DIM))
    ang = positions.reshape(-1).astype(F32)[:, None] * inv_freq
    cos, sin = jnp.cos(ang), jnp.sin(ang)
    n = ang.shape[0]
    pad1 = jnp.ones((n, HEAD_DIM - ROT_DIM), F32)
    pad0 = jnp.zeros((n, HEAD_DIM - ROT_DIM), F32)
    c = jnp.concatenate([cos, cos, pad1], axis=1)
    s = jnp.concatenate([-sin, sin, pad0], axis=1)
    return jnp.tile(c, (1, LANES // HEAD_DIM)), jnp.tile(s, (1, LANES // HEAD_DIM))


def _swa_kernel(sink_ref, q_ref, kp_ref, kc_ref, vp_ref, vc_ref, o_ref, *, group):
    n = pl.program_id(1)
    blk = q_ref.shape[0]
    scale = HEAD_DIM ** -0.5
    rows = group * blk
    qi = lax.broadcasted_iota(jnp.int32, (rows, 2 * blk), 0) % blk
    kj = lax.broadcasted_iota(jnp.int32, (rows, 2 * blk), 1)
    valid = (kj > qi) & (kj <= qi + WINDOW) & ((kj >= blk) | (n > 0))
    rid = lax.broadcasted_iota(jnp.int32, (rows, 1), 0) // blk
    KV = range(ATT_KV_HEADS)
    sls = [slice(kv * HEAD_DIM, (kv + 1) * HEAD_DIM) for kv in KV]
    heads = [[kv * group + g for g in range(group)] for kv in KV]
    s = []
    for kv in KV:
        kk = jnp.concatenate([kp_ref[:, sls[kv]], kc_ref[:, sls[kv]]], axis=0)
        qq = jnp.concatenate([q_ref[:, h * HEAD_DIM:(h + 1) * HEAD_DIM] for h in heads[kv]], axis=0)
        s.append(lax.dot_general(qq, kk, (((1,), (1,)), ((), ())), preferred_element_type=F32))
    p, inv = [], []
    for kv in KV:
        sc = jnp.where(valid, s[kv] * scale, -1e30)
        sink = jnp.zeros((rows, 1), F32)
        for g, h in enumerate(heads[kv]):
            sink = jnp.where(rid == g, sink_ref[h], sink)
        m = jnp.maximum(jnp.max(sc, axis=-1, keepdims=True), sink)
        e = jnp.exp(sc - m)
        inv.append(1.0 / (jnp.sum(e, axis=-1, keepdims=True) + jnp.exp(sink - m)))
        p.append(e.astype(BF16))
    for kv in KV:
        vv = jnp.concatenate([vp_ref[:, sls[kv]], vc_ref[:, sls[kv]]], axis=0)
        o = jnp.dot(p[kv], vv, preferred_element_type=F32) * inv[kv]
        for g, h in enumerate(heads[kv]):
            o_ref[:, h * HEAD_DIM:(h + 1) * HEAD_DIM] = o[g * blk:(g + 1) * blk].astype(o_ref.dtype)


def _swa(q, k, v, sinks, batch, seq):
    n, wq = q.shape
    wk = k.shape[1]
    blk = WINDOW
    nb = seq // blk
    group = (wq // HEAD_DIM) // ATT_KV_HEADS
    cur = lambda w: pl.BlockSpec((blk, w), lambda b, i: (b * nb + i, 0))
    prev = lambda w: pl.BlockSpec((blk, w), lambda b, i: (b * nb + jnp.maximum(i - 1, 0), 0))
    return pl.pallas_call(
        functools.partial(_swa_kernel, group=group),
        grid=(batch, nb),
        in_specs=[pl.BlockSpec(memory_space=pltpu.SMEM), cur(wq), prev(wk), cur(wk), prev(wk), cur(wk)],
        out_specs=cur(wq),
        out_shape=jax.ShapeDtypeStruct((n, wq), BF16),
        compiler_params=_cparams("parallel", "parallel"),
        name="swa",
    )(sinks.astype(F32), q, k, k, v, v)


RW_CHUNK = 64
RW_SEQS = 2


def _mm(a, b):
    return jnp.dot(a.astype(BF16), b.astype(BF16), preferred_element_type=F32)


def _mm_nt(a, b):
    return lax.dot_general(a.astype(BF16), b.astype(BF16), (((1,), (1,)), ((), ())), preferred_element_type=F32)


def _mm_tn(a, b):
    return lax.dot_general(a.astype(BF16), b.astype(BF16), (((0,), (0,)), ((), ())), preferred_element_type=F32)


def _hi_lo(a):
    hi = a.astype(BF16)
    return hi, (a - hi.astype(F32)).astype(BF16)


def _split_dot(a, b01):
    hi, lo = _hi_lo(a)
    return jnp.dot(hi, b01, preferred_element_type=F32) + jnp.dot(lo, b01, preferred_element_type=F32)


def _split_dot_left(a01, b):
    hi, lo = _hi_lo(b)
    return jnp.dot(a01, hi, preferred_element_type=F32) + jnp.dot(a01, lo, preferred_element_type=F32)


def _rwkv_kernel(p_ref, mu_ref, w0_ref, w2_ref, a0_ref, a2_ref, g2_ref, kk_ref, ka_ref, rk_ref, lnw_ref, lnb_ref,
                 o_ref, last_ref, st_ref, *, width):
    c = pl.program_id(1)
    NB, C, wp = p_ref.shape
    R = NB * C
    W = width
    N = RWKV_HEAD_DIM
    SL = 2 * N
    n_slab = W // SL

    @pl.when(c == 0)
    def _():
        last_ref[...] = jnp.zeros_like(last_ref)
        st_ref[...] = jnp.zeros_like(st_ref)

    p = p_ref[...].reshape(R, wp)
    row = lax.broadcasted_iota(jnp.int32, (R, 1), 0)
    carry = jnp.concatenate([jnp.broadcast_to(last_ref[b:b + 1, :], (C, wp)) for b in range(NB)], axis=0)
    prev = jnp.where(row % C == 0, carry, pltpu.roll(p, 1, axis=0))
    for b in range(NB):
        last_ref[b:b + 1, :] = p[(b + 1) * C - 1:(b + 1) * C, :]
    p = p + (prev - p) * mu_ref[...]
    r = p[:, :W]
    k = p[:, W:2 * W]
    v = p[:, 2 * W:3 * W]
    o3 = 3 * W
    dw = p[:, o3:o3 + DECAY_LORA]
    da = p[:, o3 + DECAY_LORA:o3 + DECAY_LORA + AAA_LORA]
    dg = p[:, o3 + DECAY_LORA + AAA_LORA:]

    z = -(w0_ref[...] + _mm(jnp.tanh(dw), w2_ref[...]))
    w_log = -(jnp.maximum(z, 0.0) + jnp.log(1.0 + jnp.exp(-jnp.abs(z)))) - 0.5
    lw = -jnp.exp(w_log)
    a = _sigmoid(a0_ref[...] + _mm(da, a2_ref[...]))
    g = _mm(_sigmoid(dg), g2_ref[...])

    ti = lax.broadcasted_iota(jnp.int32, (R, R), 0)
    tj = lax.broadcasted_iota(jnp.int32, (R, R), 1)
    tri = ((ti >= tj) & (ti // C == tj // C)).astype(BF16)
    cum = _split_dot_left(tri, lw)
    e_pos = jnp.exp(cum)
    e_neg = jnp.exp(-cum)
    e_prev = jnp.exp(cum - lw)

    li = lax.broadcasted_iota(jnp.int32, (SL, SL), 0)
    lj = lax.broadcasted_iota(jnp.int32, (SL, SL), 1)
    same = (li // N) == (lj // N)
    head_sum = same.astype(BF16)
    strict = same & ((li % N) > (lj % N))
    incl = same & ((li % N) >= (lj % N))
    lane = lax.broadcasted_iota(jnp.int32, (C, SL), 1)
    m0 = lane < N

    def stack(t):
        return jnp.concatenate([jnp.where(m0, t, 0.0), jnp.where(m0, 0.0, t)], axis=0)

    items = [(b, s) for b in range(NB) for s in range(n_slab)]
    S = range(len(items))
    at = [(slice(b * C, (b + 1) * C), slice(s * SL, (s + 1) * SL)) for b, s in items]
    lanes = [sl for _, sl in at]
    h2 = 2 * C
    bf = lambda t: t.astype(BF16)
    e_end = [jnp.exp(cum[rw.stop - 1:rw.stop, sl] - cum[rw, sl]) for rw, sl in at]
    x = [k[i] * kk_ref[:, i[1]] for i in at]
    k2 = [k[i] * (1.0 + (a[i] - 1.0) * ka_ref[:, i[1]]) for i in at]
    sums = [_split_dot(jnp.concatenate([x[s] * x[s], r[at[s]] * k2[s] * rk_ref[:, lanes[s]]], axis=0), head_sum)
            for s in S]
    kk = [x[s] / jnp.maximum(jnp.sqrt(sums[s][:C]), 1e-12) for s in S]
    kb = [kk[s] * a[at[s]] for s in S]
    AR = [bf(jnp.concatenate([stack(-kk[s] * e_prev[at[s]]), stack(r[at[s]] * e_pos[at[s]])], axis=0))
          for s in S]
    twice = lambda t: jnp.concatenate([t, t], axis=0)
    BK = [bf(jnp.concatenate([twice(kb[s] * e_neg[at[s]]), twice(k2[s] * e_neg[at[s]])], axis=0)) for s in S]
    V2 = [bf(stack(v[i])) for i in at]
    Be = [bf(twice(kb[s] * e_end[s])) for s in S]
    Ke = [bf(twice(k2[s] * e_end[s])) for s in S]
    st = [st_ref[s] for s in S]
    G = [_mm_nt(AR[s], BK[s]) for s in S]
    ST = [_mm(AR[s], st[s]) for s in S]
    A_k = [bf(jnp.concatenate([jnp.where(strict, G[s][:h2, h2:], 0.0), jnp.where(incl, G[s][h2:, h2:], 0.0)],
                              axis=0)) for s in S]
    AV = [_mm(A_k[s], V2[s]) for s in S]
    U = [ST[s][:h2] + AV[s][:h2] for s in S]
    Ap = [bf(jnp.where(strict, G[s][:h2, :h2], 0.0)) for s in S]
    steps = int(math.log2(C))
    for i in range(steps - 1):
        Rm = [_mm(Ap[s], jnp.concatenate([Ap[s], bf(U[s])], axis=1)) for s in S]
        Ap = [bf(Rm[s][:, :SL]) for s in S]
        U = [U[s] + Rm[s][:, SL:] for s in S]
    U = [U[s] + _mm(Ap[s], U[s]) for s in S]
    Ub = [bf(U[s]) for s in S]
    A_rb = [bf(jnp.where(incl, G[s][h2:, :h2], 0.0)) for s in S]
    O2 = [ST[s][h2:] + AV[s][h2:] + _mm(A_rb[s], Ub[s]) for s in S]
    for s in S:
        rw, sl = at[s]
        decay = jnp.transpose(jnp.broadcast_to(e_pos[rw.stop - 1:rw.stop, sl], (SL, SL)))
        st_new = st[s] * decay + _mm_tn(Be[s], Ub[s]) + _mm_tn(Ke[s], V2[s])
        st_ref[s] = jnp.where(same, st_new, 0.0)
    o = [O2[s][:C] + O2[s][C:] for s in S]
    mean = [_split_dot(o[s], head_sum) * (1.0 / N) for s in S]
    dlt = [o[s] - mean[s] for s in S]
    var = [_split_dot(dlt[s] * dlt[s], head_sum) * (1.0 / N) for s in S]
    bonus = [sums[s][C:] * v[at[s]] for s in S]
    for s in S:
        (b, _), sl = items[s], lanes[s]
        on = dlt[s] * lax.rsqrt(var[s] + RWKV_GN_EPS) * lnw_ref[:, sl] + lnb_ref[:, sl]
        o_ref[b, :, sl] = ((on + bonus[s]) * g[at[s]]).astype(o_ref.dtype)


def _rwkv(p, mu, w0, w2, a0, a2, g2, k_k, k_a, r_k, ln_w, ln_b, batch, seq):
    n, wp = p.shape
    width = w0.shape[-1]
    C = RW_CHUNK
    nb = RW_SEQS
    vec = lambda t: t.reshape(1, -1).astype(F32)
    args = [vec(mu), vec(w0), w2.astype(BF16), vec(a0), a2.astype(BF16), g2.astype(BF16), vec(k_k), vec(k_a),
            vec(r_k), vec(ln_w), vec(ln_b)]
    out = pl.pallas_call(
        functools.partial(_rwkv_kernel, width=width),
        grid=(batch // nb, seq // C),
        in_specs=[pl.BlockSpec((nb, C, wp), lambda b, c: (b, c, 0))] + [_const_spec(t.shape) for t in args],
        out_specs=pl.BlockSpec((nb, C, width), lambda b, c: (b, c, 0)),
        out_shape=jax.ShapeDtypeStruct((batch, seq, width), BF16),
        scratch_shapes=[pltpu.VMEM((nb, wp), F32),
                        pltpu.VMEM((nb * width // (2 * RWKV_HEAD_DIM), 2 * RWKV_HEAD_DIM, 2 * RWKV_HEAD_DIM), F32)],
        compiler_params=_cparams("parallel", "arbitrary"),
        name="rwkv",
    )(p.reshape(batch, seq, wp), *args)
    return out.reshape(n, width)


def _memkv_kernel(m_ref, g_ref, w_ref, o_ref):
    o_ref[...] = jnp.dot(_rms(m_ref[...], g_ref[...]).astype(BF16), w_ref[...],
                         preferred_element_type=F32).astype(o_ref.dtype)


def _mem_kv(mem2, g_mem, w_kv_b, mem_len):
    n, d = mem2.shape
    wo = w_kv_b.shape[1]
    return pl.pallas_call(
        _memkv_kernel,
        grid=(n // mem_len,),
        in_specs=[pl.BlockSpec((mem_len, d), lambda i: (i, 0)), _const_spec((1, d)), _const_spec(w_kv_b.shape)],
        out_specs=pl.BlockSpec((mem_len, wo), lambda i: (i, 0)),
        out_shape=jax.ShapeDtypeStruct((n, wo), BF16),
        compiler_params=_cparams("parallel"),
        name="mem_kv",
    )(mem2, g_mem.reshape(1, d), w_kv_b)


def _post_kernel(x_ref, at_ref, rw_ref, gt_ref, kv_ref, wo_ref, gc_ref, wq_ref, woc_ref, gf_ref, wpq_ref,
                 k1_ref, k2_ref, h_ref, s1_ref, s2_ref):
    d = x_ref.shape[1]
    ga = gt_ref[:, :d].astype(F32)
    gb = gt_ref[:, d:].astype(F32)
    mixed = ga * at_ref[...].astype(F32) + gb * rw_ref[...].astype(F32)
    h1 = x_ref[...] + jnp.dot(mixed.astype(BF16), wo_ref[...], preferred_element_type=F32)

    qc = jnp.dot(_rms(h1, gc_ref[...]).astype(BF16), wq_ref[...], preferred_element_type=F32)
    wc = qc.shape[1]
    hd = wc // CROSS_HEADS
    scale = hd ** -0.5
    H = range(CROSS_HEADS)
    qb = (qc * scale).astype(BF16)
    s = [lax.dot_general(qb[:, hh * hd:(hh + 1) * hd], kv_ref[:, hh * hd:(hh + 1) * hd],
                         (((1,), (1,)), ((), ())), preferred_element_type=F32) for hh in H]
    ex = [jnp.exp(s[hh] - jnp.max(s[hh], axis=-1, keepdims=True)) for hh in H]
    inv = [1.0 / jnp.sum(ex[hh], axis=-1, keepdims=True) for hh in H]
    outs = [jnp.dot(ex[hh].astype(BF16), kv_ref[:, wc + hh * hd:wc + (hh + 1) * hd],
                    preferred_element_type=F32) * inv[hh] for hh in H]
    oc = jnp.concatenate(outs, axis=1)
    h2 = h1 + jnp.dot(oc.astype(BF16), woc_ref[...], preferred_element_type=F32)
    h_ref[...] = h2

    q3 = jnp.dot(_rms(h2, gf_ref[...]).astype(BF16), wpq_ref[...], preferred_element_type=F32)
    half = k1_ref.shape[1]
    dn = (((1,), (1,)), ((), ()))
    q_hi, q_lo = _hi_lo(q3)
    k_parts = [_hi_lo(k1_ref[...]), _hi_lo(k2_ref[...])]

    def score(kp, c0):
        qh, ql = q_hi[:, c0:c0 + half], q_lo[:, c0:c0 + half]
        return (lax.dot_general(kp[0], qh, dn, preferred_element_type=F32)
                + lax.dot_general(kp[0], ql, dn, preferred_element_type=F32)
                + lax.dot_general(kp[1], qh, dn, preferred_element_type=F32))

    for hh in range(PEER_HEADS):
        s1_ref[hh] = score(k_parts[0], 2 * hh * half)
        s2_ref[hh] = score(k_parts[1], (2 * hh + 1) * half)


def _post(x2, attn, rw, gates, kv, w_out_b, g_cross, w_qc_b, w_oc_b, g_ffn, w_pq_b, k1, k2, batch, seq, mem_len, tm):
    n, d = x2.shape
    nt = seq // tm
    n_keys = k1.shape[0]
    row = lambda w: pl.BlockSpec((tm, w), lambda b, i: (b * nt + i, 0))
    sc_spec = pl.BlockSpec((PEER_HEADS, n_keys, tm), lambda b, i: (0, 0, b * nt + i))
    consts = [w_out_b, g_cross.reshape(1, d), w_qc_b, w_oc_b, g_ffn.reshape(1, d), w_pq_b, k1.astype(F32), k2.astype(F32)]
    return pl.pallas_call(
        _post_kernel,
        grid=(batch, nt),
        in_specs=[row(d), row(d), row(d), row(2 * d), pl.BlockSpec((mem_len, kv.shape[1]), lambda b, i: (b, 0))]
                 + [_const_spec(t.shape) for t in consts],
        out_specs=[row(d), sc_spec, sc_spec],
        out_shape=[jax.ShapeDtypeStruct((n, d), F32), jax.ShapeDtypeStruct((PEER_HEADS, n_keys, n), F32),
                   jax.ShapeDtypeStruct((PEER_HEADS, n_keys, n), F32)],
        compiler_params=_cparams("parallel", "parallel"),
        name="post",
    )(x2, attn, rw, gates, kv, *consts)


ROUTE_UNROLL = 4
TABLE_ROWS_PER_EXPERT = 4


def _topk_rows(s, k, payload=None):
    rows = s.shape[0]
    rid = lax.broadcasted_iota(jnp.int32, s.shape, 0).astype(F32)
    vals, outs = [], []
    for _ in range(k):
        m = jnp.max(s, axis=0, keepdims=True)
        i = jnp.min(jnp.where(s == m, rid, float(rows)), axis=0, keepdims=True)
        hit = rid == i
        vals.append(m)
        outs.append(i if payload is None else jnp.max(jnp.where(hit, payload, -1.0), axis=0, keepdims=True))
        s = jnp.where(hit, -jnp.inf, s)
    return jnp.concatenate(vals, axis=0), jnp.concatenate(outs, axis=0)


def _route_kernel(s1_ref, s2_ref, e_ref, g_ref):
    K = PEER_TOPK
    n_keys = s1_ref.shape[1]

    def one_head(h):
        v1, i1 = _topk_rows(s1_ref[h], K)
        v2, i2 = _topk_rows(s2_ref[h], K)
        cand, ecand = [], []
        for a in range(K):
            nb = K // (a + 1)
            cand.append(v1[a:a + 1, :] + v2[:nb, :])
            ecand.append(i1[a:a + 1, :] * float(n_keys) + i2[:nb, :])
        n_c = sum(c.shape[0] for c in cand)
        pad = -n_c % SUBLANES
        if pad:
            cand.append(jnp.full((pad, v1.shape[1]), -jnp.inf, F32))
            ecand.append(jnp.full((pad, v1.shape[1]), -1.0, F32))
        sc, e = _topk_rows(jnp.concatenate(cand, axis=0), K, jnp.concatenate(ecand, axis=0))
        e_ref[h] = (e * float(TABLE_ROWS_PER_EXPERT)).astype(jnp.int32)
        ex = jnp.exp(sc - sc[0:1, :])
        g_ref[h] = ex / jnp.sum(ex, axis=0, keepdims=True)

    def body(i, carry):
        for j in range(ROUTE_UNROLL):
            one_head(i * ROUTE_UNROLL + j)
        return carry

    lax.fori_loop(0, s1_ref.shape[0] // ROUTE_UNROLL, body, 0)


def _route(s1, s2, tk):
    heads, n_keys, n = s1.shape
    spec_in = pl.BlockSpec((heads, n_keys, tk), lambda i: (0, 0, i))
    spec_out = pl.BlockSpec((heads, PEER_TOPK, tk), lambda i: (0, 0, i))
    return pl.pallas_call(
        _route_kernel,
        grid=(n // tk,),
        in_specs=[spec_in, spec_in],
        out_specs=[spec_out, spec_out],
        out_shape=[jax.ShapeDtypeStruct((heads, PEER_TOPK, n), jnp.int32),
                   jax.ShapeDtypeStruct((heads, PEER_TOPK, n), F32)],
        compiler_params=_cparams("parallel"),
        name="route",
    )(s1, s2)


PEER_GROUP = 16
SUB = SUBLANES
PEER_UNROLL = 32


def _gather_rows(tbl_ref, e_ref, t, g):
    rows = TABLE_ROWS_PER_EXPERT
    parts = [tbl_ref[pl.ds(pl.multiple_of(e_ref.at[g * PEER_GROUP + j][t], rows), rows), :]
             for j in range(PEER_GROUP)]
    return pltpu.bitcast(jnp.concatenate(parts, axis=0), BF16)


def _diag_mask(shape):
    r = lax.broadcasted_iota(jnp.int32, shape, len(shape) - 2)
    c = lax.broadcasted_iota(jnp.int32, shape, len(shape) - 1)
    return r == (c % SUB)


def _with_index_tile(e_hbm, bufs, sems, tm, fn):
    i = pl.program_id(0)

    def copy(step, slot):
        return pltpu.make_async_copy(e_hbm.at[:, pl.ds(pl.multiple_of(step * tm, tm), tm)], bufs[slot], sems.at[slot])

    @pl.when(i == 0)
    def _():
        copy(0, 0).start()

    for slot in range(2):
        @pl.when(i % 2 == slot)
        def _(slot=slot):
            @pl.when(i + 1 < pl.num_programs(0))
            def _():
                copy(i + 1, 1 - slot).start()

            copy(i, slot).wait()
            fn(bufs[slot])


def _peer_act_kernel(e_hbm, h_ref, gf_ref, gate_ref, tbl_ref, o_ref, act_ref, x2_ref, e_a, e_b, sems):
    tm = h_ref.shape[0]
    n_grp = e_a.shape[0] // PEER_GROUP
    xn = _rms(h_ref[...], gf_ref[...]).reshape(tm, SUB, LANES).astype(BF16)
    x2_ref[...] = jnp.concatenate([xn, xn], axis=1)
    kr = lax.broadcasted_iota(jnp.int32, (PEER_GROUP, PEER_GROUP * SUB), 0)
    kc = lax.broadcasted_iota(jnp.int32, (PEER_GROUP, PEER_GROUP * SUB), 1) // SUB
    row_sum = (kr == kc).astype(BF16)
    sub = lax.broadcasted_iota(jnp.int32, (SUB, LANES), 0)

    def gather_loop(e_ref):
        def body(i, carry):
            qs = []
            for j in range(PEER_UNROLL):
                t = i * PEER_UNROLL + j
                xt = jnp.tile(x2_ref[t], (PEER_GROUP // 2, 1))
                qs.append(jnp.concatenate([jnp.dot(row_sum, _gather_rows(tbl_ref, e_ref, t, g) * xt,
                                                   preferred_element_type=F32) for g in range(n_grp)], axis=0))
            for jb in range(0, PEER_UNROLL, SUB):
                blk = jnp.zeros((SUB, LANES), F32)
                for j in range(SUB):
                    dots = jnp.sum(qs[jb + j].T, axis=0, keepdims=True)
                    blk = jnp.where(sub == j, dots, blk)
                act_ref[pl.ds(pl.multiple_of(i * PEER_UNROLL + jb, SUB), SUB), :] = blk
            return carry

        lax.fori_loop(0, tm // PEER_UNROLL, body, 0)

    _with_index_tile(e_hbm, (e_a, e_b), sems, tm, gather_loop)
    act = act_ref[...]
    gelu = 0.5 * act * (1.0 + lax.erf(act * (2.0 ** -0.5)))
    o_ref[...] = gate_ref[...].T * gelu


def _peer_out_kernel(e_hbm, h_ref, gw_ref, tbl_ref, exp_ref, gfin_ref, o_ref, ge_ref, acc_ref, e_a, e_b, sems):
    tm = h_ref.shape[0]
    n_grp = e_a.shape[0] // PEER_GROUP
    ge_ref[...] = _split_dot(gw_ref[...], exp_ref[...])
    mask = _diag_mask((SUB, n_grp * LANES))

    def gather_loop(e_ref):
        def body(i, carry):
            for j in range(PEER_UNROLL):
                t = i * PEER_UNROLL + j
                lhs = jnp.where(mask, jnp.broadcast_to(ge_ref[pl.ds(t, 1), :], mask.shape), 0.0).astype(BF16)
                acc = jnp.zeros((SUB, LANES), F32)
                for g in range(n_grp):
                    w = _gather_rows(tbl_ref, e_ref, t, g)
                    acc = acc + jnp.dot(lhs[:, g * LANES:(g + 1) * LANES], w, preferred_element_type=F32)
                acc_ref[t] = acc
            return carry

        lax.fori_loop(0, tm // PEER_UNROLL, body, 0)

    _with_index_tile(e_hbm, (e_a, e_b), sems, tm, gather_loop)
    o_ref[...] = _rms(h_ref[...] + acc_ref[...].reshape(o_ref.shape), gfin_ref[...])


def _table_tiles(t):
    e, d = t.shape
    p = d // (2 * LANES)
    te = 512
    return pl.pallas_call(
        functools.partial(_pack_kernel, rows=p),
        grid=(e // te,),
        in_specs=[pl.BlockSpec((te, d), lambda i: (i, 0))],
        out_specs=pl.BlockSpec((te * p, LANES), lambda i: (i, 0)),
        out_shape=jax.ShapeDtypeStruct((e * p, LANES), jnp.int32),
        compiler_params=_cparams("parallel"),
        name="pack_table",
    )(t)


def _pack_kernel(t_ref, o_ref, *, rows):
    te = t_ref.shape[0]
    for p in range(rows):
        halves = []
        for c in range(2):
            x = t_ref[:, (2 * p + c) * LANES:(2 * p + c + 1) * LANES]
            halves.append(pltpu.bitcast(x.astype(BF16).astype(F32), jnp.int32))
        word = lax.shift_right_logical(halves[0], 16) | halves[1]
        o_ref[pl.ds(p, te, stride=rows), :] = word


def _peer(h2, e_idx, gate, g_ffn, g_final, u_tiles, v_tiles, tm):
    n, d = h2.shape
    kk = e_idx.shape[0]
    rows = pl.BlockSpec((tm, d), lambda i: (i, 0))
    row = pl.BlockSpec((tm, kk), lambda i: (i, 0))
    lane = jnp.arange(kk * SUB) // SUB
    expand = (jnp.arange(kk)[:, None] == lane[None, :]).astype(BF16)
    index_scratch = [pltpu.SMEM((kk, tm), jnp.int32), pltpu.SMEM((kk, tm), jnp.int32), pltpu.SemaphoreType.DMA((2,))]
    gw = pl.pallas_call(
        _peer_act_kernel,
        grid=(n // tm,),
        in_specs=[pl.BlockSpec(memory_space=pl.ANY), rows, _const_spec((1, d)),
                  pl.BlockSpec((kk, tm), lambda i: (0, i)), _const_spec(u_tiles.shape)],
        out_specs=row,
        out_shape=jax.ShapeDtypeStruct((n, kk), F32),
        scratch_shapes=[pltpu.VMEM((tm, kk), F32), pltpu.VMEM((tm, 2 * SUB, d // SUB), BF16)] + index_scratch,
        compiler_params=_cparams("arbitrary"),
        name="peer_act",
    )(e_idx, h2, g_ffn.reshape(1, d), gate, u_tiles)
    return pl.pallas_call(
        _peer_out_kernel,
        grid=(n // tm,),
        in_specs=[pl.BlockSpec(memory_space=pl.ANY), rows, row, _const_spec(v_tiles.shape), _const_spec(expand.shape),
                  _const_spec((1, d))],
        out_specs=rows,
        out_shape=jax.ShapeDtypeStruct((n, d), F32),
        scratch_shapes=[pltpu.VMEM((tm, kk * SUB), F32), pltpu.VMEM((tm, SUB, d // SUB), F32)] + index_scratch,
        compiler_params=_cparams("arbitrary"),
        name="peer_out",
    )(e_idx, h2, gw, v_tiles, expand, g_final.reshape(1, d))


def kernel(x, mem, positions, g_mix, w_in, attn_sinks, rwkv_mu, rwkv_w0, rwkv_w2, rwkv_a0, rwkv_a2, rwkv_g2, rwkv_k_k, rwkv_k_a, rwkv_r_k, rwkv_ln_w, rwkv_ln_b, w_out, g_cross, g_mem, w_q_cross, w_kv_cross, w_o_cross, g_ffn, peer_w_q, peer_sub_keys_1, peer_sub_keys_2, peer_u, peer_v, g_final):
    b, s, d = x.shape
    n = b * s
    L = 0
    wq = d
    wk = ATT_KV_HEADS * HEAD_DIM
    wp = 3 * d + DECAY_LORA + AAA_LORA + GATE_LORA
    widths = (wq, wk, wk, wp, 2 * d)
    rc, rs = _rope_tables(positions)
    q, k, v, p, gates = _in_proj(x.reshape(n, d), g_mix[L], w_in[L].astype(BF16), rc, rs, widths, 256)
    attn = _swa(q, k, v, attn_sinks[L], b, s)
    rw = _rwkv(p, rwkv_mu[L], rwkv_w0[L], rwkv_w2[L], rwkv_a0[L], rwkv_a2[L], rwkv_g2[L], rwkv_k_k[L], rwkv_k_a[L],
               rwkv_r_k[L], rwkv_ln_w[L], rwkv_ln_b[L], b, s)
    m = mem.shape[1]
    kv = _mem_kv(mem.reshape(b * m, d), g_mem[L], w_kv_cross[L].astype(BF16), m)
    h2, s1, s2 = _post(x.reshape(n, d), attn, rw, gates, kv, w_out[L].astype(BF16), g_cross[L],
                       w_q_cross[L].astype(BF16), w_o_cross[L].astype(BF16), g_ffn[L], peer_w_q[L].astype(BF16),
                       peer_sub_keys_1[L], peer_sub_keys_2[L], b, s, m, 256)
    e_t, g_t = _route(s1, s2, LANES)
    kk = PEER_HEADS * PEER_TOPK
    assert d == TABLE_ROWS_PER_EXPERT * 2 * LANES
    out = _peer(h2, e_t.reshape(kk, n), g_t.reshape(kk, n), g_ffn[L], g_final, _table_tiles(peer_u[L]),
                _table_tiles(peer_v[L]), LANES)
    return out.reshape(b, s, d)
```
